```python
import math
import jax, jax.numpy as jnp
from jax import lax
import numpy as np

D_MODEL = 1024
BATCH = 4
SEQ = 8192
DEPTH = 1

MLSTM_HEADS = 4
MLSTM_WIDTH = D_MODEL
MLSTM_HEAD_DIM = MLSTM_WIDTH // MLSTM_HEADS
MLSTM_CHUNK = 128
CONV_WIDTH = 4
S5_WIDTH = D_MODEL // 2
S5_GROUP = 16
S5_GROUPS = S5_WIDTH // S5_GROUP
S5_STATE = 64
DT_MIN = 0.001
DT_MAX = 0.1
N_BRANCHES = 2
NORM_EPS = 1e-6
IN_TOTAL = 3 * MLSTM_WIDTH + 2 * MLSTM_HEADS + 2 * S5_WIDTH + N_BRANCHES * D_MODEL

kernel_name = "hybrid_mlstm_s5_gated_block"


def _split_points():
    sizes = (MLSTM_WIDTH, MLSTM_WIDTH, MLSTM_WIDTH, MLSTM_HEADS, MLSTM_HEADS,
             S5_WIDTH, S5_WIDTH, N_BRANCHES * D_MODEL)
    return [int(s) for s in np.cumsum(sizes)[:-1]]


def rmsnorm(x, g):
    xf = x.astype(jnp.float32)
    y = xf * lax.rsqrt(jnp.mean(xf * xf, axis=-1, keepdims=True) + NORM_EPS)
    return (y * g.astype(jnp.float32)).astype(x.dtype)


def causal_dwconv(u, w, b):
    K, C = w.shape
    y = lax.conv_general_dilated(u, w[:, None, :].astype(u.dtype), window_strides=(1,),
                                 padding=[(K - 1, 0)], dimension_numbers=('NWC', 'WIO', 'NWC'),
                                 feature_group_count=C)
    return y + b.astype(u.dtype)


def mlstm_chunkwise(q, k, v, ig, lf):
    Bsz, H, L, dh = q.shape
    Lc = MLSTM_CHUNK
    NC = L // Lc
    q = q.reshape(Bsz, H, NC, Lc, dh)
    k = k.reshape(Bsz, H, NC, Lc, dh) * (dh ** -0.5)
    v = v.reshape(Bsz, H, NC, Lc, dh)
    ig = ig.reshape(Bsz, H, NC, Lc)
    b = jnp.cumsum(lf.reshape(Bsz, H, NC, Lc), axis=-1)
    b_last = b[..., -1]
    w_end = b_last[..., None] - b + ig
    m_loc = jnp.max(w_end, axis=-1)
    e_end = jnp.exp(w_end - m_loc[..., None])
    C_loc = jnp.einsum('bhcs,bhcsd,bhcse->bhcde', e_end, v, k)
    n_loc = jnp.einsum('bhcs,bhcse->bhce', e_end, k)

    def step(carry, inp):
        C, n, m = carry
        C_l, n_l, m_l, bl = inp
        m_new = jnp.maximum(bl + m, m_l)
        a = jnp.exp(bl + m - m_new)
        s = jnp.exp(m_l - m_new)
        C_new = a[..., None, None] * C + s[..., None, None] * C_l
        n_new = a[..., None] * n + s[..., None] * n_l
        return (C_new, n_new, m_new), (C, n, m)

    init = (jnp.zeros((Bsz, H, dh, dh), jnp.float32), jnp.zeros((Bsz, H, dh), jnp.float32),
            jnp.full((Bsz, H), -jnp.inf, jnp.float32))
    xs = (jnp.moveaxis(C_loc, 2, 0), jnp.moveaxis(n_loc, 2, 0),
          jnp.moveaxis(m_loc, 2, 0), jnp.moveaxis(b_last, 2, 0))
    _, (C_prev, n_prev, m_prev) = lax.scan(step, init, xs)
    C_prev = jnp.moveaxis(C_prev, 0, 2)
    n_prev = jnp.moveaxis(n_prev, 0, 2)
    m_prev = jnp.moveaxis(m_prev, 0, 2)

    causal = jnp.tril(jnp.ones((Lc, Lc), dtype=bool))
    log_d = jnp.where(causal, b[..., :, None] - b[..., None, :] + ig[..., None, :], -jnp.inf)
    log_inter = b + m_prev[..., None]
    m_t = jnp.maximum(log_inter, jnp.max(log_d, axis=-1))
    dmat = jnp.exp(log_d - m_t[..., None])
    inter = jnp.exp(log_inter - m_t)
    s = jnp.einsum('bhctd,bhcsd->bhcts', q, k) * dmat
    num = jnp.einsum('bhcts,bhcsd->bhctd', s, v) + inter[..., None] * jnp.einsum('bhcde,bhcte->bhctd', C_prev, q)
    den = jnp.sum(s, axis=-1) + inter * jnp.einsum('bhce,bhcte->bhct', n_prev, q)
    h = num / jnp.maximum(jnp.abs(den), jnp.exp(-m_t))[..., None]
    return h.reshape(Bsz, H, L, dh)


def mlstm_branch(u_a, z_a, o_pre, i_pre, f_pre, conv_w, conv_b, w_q, w_k, w_v, b_i, b_f, head_g, skip):
    Bsz, L, _ = u_a.shape
    H, dh = MLSTM_HEADS, MLSTM_HEAD_DIM
    c = jax.nn.silu(causal_dwconv(u_a, conv_w, conv_b))
    ch = c.reshape(Bsz, L, H, dh)
    uh = u_a.reshape(Bsz, L, H, dh)
    q = jnp.einsum('blhd,hde->bhle', ch, w_q).astype(jnp.float32)
    k = jnp.einsum('blhd,hde->bhle', ch, w_k).astype(jnp.float32)
    v = jnp.einsum('blhd,hde->bhle', uh, w_v).astype(jnp.float32)
    ig = jnp.transpose(i_pre.astype(jnp.float32) + b_i.astype(jnp.float32), (0, 2, 1))
    lf = jax.nn.log_sigmoid(jnp.transpose(f_pre.astype(jnp.float32) + b_f.astype(jnp.float32), (0, 2, 1)))
    h = mlstm_chunkwise(q, k, v, ig, lf)
    o = jnp.transpose(jax.nn.sigmoid(o_pre.astype(jnp.float32)).reshape(Bsz, L, H, dh), (0, 2, 1, 3))
    h = o * h
    h = h * lax.rsqrt(jnp.mean(h * h, axis=-1, keepdims=True) + NORM_EPS)
    h = jnp.transpose(h, (0, 2, 1, 3)).reshape(Bsz, L, MLSTM_WIDTH) * head_g.astype(jnp.float32)
    h = h.astype(u_a.dtype) + skip * c
    return h * jax.nn.silu(z_a)


def s5_branch(u_b, z_b, lam_re, lam_im, log_dt, B_re, B_im, C_re, C_im, D_skip, w_glu, b_glu):
    Bsz, L, _ = u_b.shape
    u = u_b.astype(jnp.float32).reshape(Bsz, L, S5_GROUPS, S5_GROUP)
    dt = jnp.exp(log_dt.astype(jnp.float32))[:, None]
    lr = lam_re.astype(jnp.float32)
    li = lam_im.astype(jnp.float32)
    mag = jnp.exp(lr * dt)
    a_re = mag * jnp.cos(li * dt)
    a_im = mag * jnp.sin(li * dt)
    den = lr * lr + li * li
    nr = a_re - 1.0
    q_re = (nr * lr + a_im * li) / den
    q_im = (a_im * lr - nr * li) / den
    Br = B_re.astype(jnp.float32)
    Bi = B_im.astype(jnp.float32)
    bb_re = q_re[..., None] * Br - q_im[..., None] * Bi
    bb_im = q_re[..., None] * Bi + q_im[..., None] * Br
    bu_re = jnp.einsum('blgn,gpn->blgp', u, bb_re)
    bu_im = jnp.einsum('blgn,gpn->blgp', u, bb_im)
    a_re_b = jnp.broadcast_to(a_re, bu_re.shape)
    a_im_b = jnp.broadcast_to(a_im, bu_re.shape)

    def combine(e1, e2):
        a1r, a1i, b1r, b1i = e1
        a2r, a2i, b2r, b2i = e2
        return (a2r * a1r - a2i * a1i, a2r * a1i + a2i * a1r,
                a2r * b1r - a2i * b1i + b2r, a2r * b1i + a2i * b1r + b2i)

    _, _, s_re, s_im = lax.associative_scan(combine, (a_re_b, a_im_b, bu_re, bu_im), axis=1)
    y = (jnp.einsum('blgp,gnp->blgn', s_re, C_re.astype(jnp.float32))
         - jnp.einsum('blgp,gnp->blgn', s_im, C_im.astype(jnp.float32))
         + D_skip.astype(jnp.float32) * u)
    y = jax.nn.gelu(y.reshape(Bsz, L, S5_WIDTH)).astype(u_b.dtype)
    y = y * jax.nn.sigmoid(y @ w_glu + b_glu)
    return y * jax.nn.silu(z_b)


def setup_inputs(seed: int = 0) -> dict:
    key = jax.random.key(seed)
    ks = jax.random.split(key, 32)
    f32 = jnp.float32
    nrm = lambda k, shape, scale: scale * jax.random.normal(k, shape, f32)
    H, dh, P, G, N = MLSTM_HEADS, MLSTM_HEAD_DIM, S5_STATE, S5_GROUPS, S5_GROUP
    x = jax.random.normal(ks[0], (BATCH, SEQ, D_MODEL), f32)
    norm_pre_g = 1.0 + nrm(ks[1], (DEPTH, D_MODEL), 0.02)
    w_in = nrm(ks[2], (DEPTH, D_MODEL, IN_TOTAL), D_MODEL ** -0.5)
    conv_w = nrm(ks[3], (DEPTH, CONV_WIDTH, MLSTM_WIDTH), CONV_WIDTH ** -0.5)
    conv_b = nrm(ks[4], (DEPTH, MLSTM_WIDTH), 0.01)
    w_q = nrm(ks[5], (DEPTH, H, dh, dh), dh ** -0.5)
    w_k = nrm(ks[6], (DEPTH, H, dh, dh), dh ** -0.5)
    w_v = nrm(ks[7], (DEPTH, H, dh, dh), dh ** -0.5)
    b_i = nrm(ks[8], (DEPTH, H), 0.1)
    b_f = 3.0 + 3.0 * jax.random.uniform(ks[9], (DEPTH, H), f32)
    head_g = 1.0 + nrm(ks[10], (DEPTH, MLSTM_WIDTH), 0.02)
    skip_a = 1.0 + nrm(ks[11], (DEPTH, MLSTM_WIDTH), 0.02)
    w_a_out = nrm(ks[12], (DEPTH, MLSTM_WIDTH, D_MODEL), MLSTM_WIDTH ** -0.5)
    lam_re = -0.5 + nrm(ks[13], (DEPTH, G, P), 0.01)
    lam_im = math.pi * jnp.arange(P, dtype=f32)[None, None, :] + nrm(ks[14], (DEPTH, G, P), 0.01)
    log_dt = math.log(DT_MIN) + (math.log(DT_MAX) - math.log(DT_MIN)) * jax.random.uniform(ks[15], (DEPTH, G), f32)
    B_re = nrm(ks[16], (DEPTH, G, P, N), (2.0 * N) ** -0.5)
    B_im = nrm(ks[17], (DEPTH, G, P, N), (2.0 * N) ** -0.5)
    C_re = nrm(ks[18], (DEPTH, G, N, P), P ** -0.5)
    C_im = nrm(ks[19], (DEPTH, G, N, P), P ** -0.5)
    D_skip = nrm(ks[20], (DEPTH, G, N), 1.0)
    w_glu = nrm(ks[21], (DEPTH, S5_WIDTH, S5_WIDTH), S5_WIDTH ** -0.5)
    b_glu = nrm(ks[22], (DEPTH, S5_WIDTH), 0.01)
    w_b_out = nrm(ks[23], (DEPTH, S5_WIDTH, D_MODEL), S5_WIDTH ** -0.5)
    w_o = nrm(ks[24], (DEPTH, D_MODEL, D_MODEL), D_MODEL ** -0.5)
    norm_post_g = 1.0 + nrm(ks[25], (DEPTH, D_MODEL), 0.02)
    return {"x": x, "norm_pre_g": norm_pre_g, "w_in": w_in, "conv_w": conv_w, "conv_b": conv_b,
            "w_q": w_q, "w_k": w_k, "w_v": w_v, "b_i": b_i, "b_f": b_f, "head_g": head_g,
            "skip_a": skip_a, "w_a_out": w_a_out, "lam_re": lam_re, "lam_im": lam_im,
            "log_dt": log_dt, "B_re": B_re, "B_im": B_im, "C_re": C_re, "C_im": C_im,
            "D_skip": D_skip, "w_glu": w_glu, "b_glu": b_glu, "w_b_out": w_b_out,
            "w_o": w_o, "norm_post_g": norm_post_g}


def reference(x, norm_pre_g, w_in, conv_w, conv_b, w_q, w_k, w_v, b_i, b_f, head_g, skip_a,
              w_a_out, lam_re, lam_im, log_dt, B_re, B_im, C_re, C_im, D_skip, w_glu, b_glu,
              w_b_out, w_o, norm_post_g):
    h = x
    split_points = _split_points()
    for l in range(DEPTH):
        xn = rmsnorm(h, norm_pre_g[l])
        proj = xn @ w_in[l]
        u_a, z_a, o_a, i_a, f_a, u_b, z_b, g_pre = jnp.split(proj, split_points, axis=-1)
        y_a = mlstm_branch(u_a, z_a, o_a, i_a, f_a, conv_w[l], conv_b[l], w_q[l], w_k[l], w_v[l],
                           b_i[l], b_f[l], head_g[l], skip_a[l]) @ w_a_out[l]
        y_b = s5_branch(u_b, z_b, lam_re[l], lam_im[l], log_dt[l], B_re[l], B_im[l], C_re[l],
                        C_im[l], D_skip[l], w_glu[l], b_glu[l]) @ w_b_out[l]
        g = jax.nn.sigmoid(g_pre.astype(jnp.float32)).astype(h.dtype)
        merged = g[..., :D_MODEL] * y_a + g[..., D_MODEL:] * y_b
        h = h + rmsnorm(merged @ w_o[l], norm_post_g[l])
    return h
```

```python
import functools
import math

import jax
import jax.numpy as jnp
import numpy as np
from jax import lax
from jax.experimental import pallas as pl
from jax.experimental.pallas import tpu as pltpu

F32 = jnp.float32
BF16 = jnp.bfloat16

D_MODEL = 1024
HEADS = 4
HEAD_DIM = 256
CHUNK = 128
CONV_WIDTH = 4
S5_WIDTH = 512
S5_GROUP = 16
S5_GROUPS = 32
S5_STATE = 64
S5_LANES = S5_GROUPS * S5_STATE
NORM_EPS = 1e-6
MAIN_COLS = 6144
GATE_ROWS = 16

TILE = 512
SUBSEQ = 8
SUBLEN = TILE // SUBSEQ
VMEM_LIMIT = 56 * 1024 * 1024


def _dot(a, b):
    return jnp.dot(a, b, preferred_element_type=F32)


def _dot_nt(a, b):
    return lax.dot_general(a, b, (((1,), (1,)), ((), ())), preferred_element_type=F32)


def _dot_tn(a, b):
    return lax.dot_general(a, b, (((0,), (0,)), ((), ())), preferred_element_type=F32)


def _silu(x):
    return x * jax.nn.sigmoid(x)


def _in_proj_kernel(x_ref, g_ref, w_ref, wgt_ref, proj_ref, gates_ref):
    x = x_ref[...]
    xn = x * lax.rsqrt(jnp.mean(x * x, axis=-1, keepdims=True) + NORM_EPS) * g_ref[...]
    xn = xn.astype(BF16)
    for j in range(MAIN_COLS // 1024):
        cols = slice(j * 1024, (j + 1) * 1024)
        proj_ref[:, cols] = _dot(xn, w_ref[:, cols]).astype(BF16)
    gates_ref[...] = _dot_nt(wgt_ref[...], xn)


def _in_proj(x2d, norm_g, w_main, w_gate_t):
    T = x2d.shape[0]
    const = lambda i: (0, 0)
    return pl.pallas_call(
        _in_proj_kernel,
        grid=(T // TILE,),
        in_specs=[
            pl.BlockSpec((TILE, D_MODEL), lambda i: (i, 0)),
            pl.BlockSpec((1, D_MODEL), const),
            pl.BlockSpec((D_MODEL, MAIN_COLS), const),
            pl.BlockSpec((GATE_ROWS, D_MODEL), const),
        ],
        out_specs=[
            pl.BlockSpec((TILE, MAIN_COLS), lambda i: (i, 0)),
            pl.BlockSpec((GATE_ROWS, TILE), lambda i: (0, i)),
        ],
        out_shape=[
            jax.ShapeDtypeStruct((T, MAIN_COLS), BF16),
            jax.ShapeDtypeStruct((GATE_ROWS, T), F32),
        ],
        compiler_params=pltpu.CompilerParams(
            dimension_semantics=("arbitrary",), vmem_limit_bytes=VMEM_LIMIT),
        name="in_proj",
    )(x2d, norm_g, w_main, w_gate_t)


def _lane_cumsum(x):
    lane = lax.broadcasted_iota(jnp.int32, x.shape, 1)
    d = 1
    while d < x.shape[1]:
        x = x + jnp.where(lane >= d, pltpu.roll(x, d, axis=1), 0.0)
        d *= 2
    return x


def _mlstm_kernel(ua_ref, za_ref, oa_ref, gates_ref, convw_ref, convb_ref, wq_ref, wk_ref,
                  wv_ref, gbias_ref, headg_ref, skip_ref, wout_ref, out_ref,
                  ext_ref, c_ref, q_ref, k_ref, v_ref, hfin_ref, ct_ref, n_ref, m_ref):
    i = pl.program_id(1)

    @pl.when(i == 0)
    def _():
        ext_ref[0:8, :] = jnp.zeros((8, D_MODEL), F32)
        ct_ref[...] = jnp.zeros_like(ct_ref)
        n_ref[...] = jnp.zeros_like(n_ref)
        m_ref[...] = jnp.full(m_ref.shape, -jnp.inf, F32)

    ext_ref[8:8 + TILE, :] = ua_ref[...].astype(F32)
    for h in range(HEADS):
        cols = slice(h * HEAD_DIM, (h + 1) * HEAD_DIM)
        acc = convb_ref[:, cols]
        for j in range(CONV_WIDTH):
            shift = CONV_WIDTH - 1 - j
            acc = acc + convw_ref[j:j + 1, cols] * ext_ref[8 - shift:8 - shift + TILE, cols]
        c = _silu(acc)
        c_ref[:, cols] = c
        cb = c.astype(BF16)
        q_ref[:, cols] = _dot(cb, wq_ref[h]).astype(BF16)
        k_ref[:, cols] = _dot(cb, wk_ref[h]).astype(BF16)
        v_ref[:, cols] = _dot(ua_ref[:, cols], wv_ref[h]).astype(BF16)
    ext_ref[0:8, :] = ext_ref[TILE:TILE + 8, :]

    row = lax.broadcasted_iota(jnp.int32, (CHUNK, CHUNK), 0)
    col = lax.broadcasted_iota(jnp.int32, (CHUNK, CHUNK), 1)
    eye = row == col
    causal = row >= col
    grow = lax.broadcasted_iota(jnp.int32, (GATE_ROWS, CHUNK), 0)
    glane = lax.broadcasted_iota(jnp.int32, (1, CHUNK), 1)

    for c_idx in range(TILE // CHUNK):
        rows = slice(c_idx * CHUNK, (c_idx + 1) * CHUNK)
        pre = gates_ref[:, rows] + gbias_ref[...]
        logsig = jnp.minimum(pre, 0.0) - jnp.log1p(jnp.exp(-jnp.abs(pre)))
        gate = jnp.where(grow >= HEADS, logsig, pre)
        bsum = _lane_cumsum(gate)
        for h in range(HEADS):
            cols = slice(h * HEAD_DIM, (h + 1) * HEAD_DIM)
            ig_r = gate[h:h + 1, :]
            b_r = bsum[HEADS + h:HEADS + h + 1, :]
            b_c = jnp.sum(jnp.where(eye, b_r, 0.0), axis=-1, keepdims=True)
            ig_c = jnp.sum(jnp.where(eye, ig_r, 0.0), axis=-1, keepdims=True)
            b_last = jnp.sum(jnp.where(glane == CHUNK - 1, b_r, 0.0), axis=-1, keepdims=True)
            m_prev = m_ref[h]
            q = q_ref[rows, cols]
            k = k_ref[rows, cols]
            v = v_ref[rows, cols]

            log_d = jnp.where(causal, b_c - b_r + ig_r, -jnp.inf)
            log_inter = b_c + m_prev
            m_t = jnp.maximum(log_inter, jnp.max(log_d, axis=-1, keepdims=True))
            dmat = jnp.exp(log_d - m_t)
            inter = jnp.exp(log_inter - m_t)
            s = _dot_nt(q, k) * dmat
            ct_prev = ct_ref[h]
            num = _dot(s.astype(BF16), v) + inter * _dot(q, ct_prev.astype(BF16))
            qn = jnp.sum(q.astype(F32) * n_ref[h], axis=-1, keepdims=True)
            den = jnp.sum(s, axis=-1, keepdims=True) + inter * qn
            hh = num / jnp.maximum(jnp.abs(den), jnp.exp(-m_t))

            w_end_r = b_last - b_r + ig_r
            m_loc = jnp.max(w_end_r, axis=-1, keepdims=True)
            m_new = jnp.maximum(b_last + m_prev, m_loc)
            decay = jnp.exp(b_last + m_prev - m_new)
            e_c = jnp.exp(b_last - b_c + ig_c - m_new)
            kw = k.astype(F32) * e_c
            ct_ref[h] = decay * ct_prev + _dot_tn(kw.astype(BF16), v)
            n_ref[h] = decay * n_ref[h] + jnp.sum(kw, axis=0, keepdims=True)
            m_ref[h] = m_new

            hh = jax.nn.sigmoid(oa_ref[rows, cols].astype(F32)) * hh
            hh = hh * lax.rsqrt(jnp.mean(hh * hh, axis=-1, keepdims=True) + NORM_EPS)
            hh = hh * headg_ref[:, cols] + skip_ref[:, cols] * c_ref[rows, cols]
            hh = hh * _silu(za_ref[rows, cols].astype(F32))
            hfin_ref[rows, cols] = hh.astype(BF16)

    out_ref[...] = _dot(hfin_ref[...], wout_ref[...]).astype(BF16)


def _mlstm(proj, gates, conv_w, conv_b, wq, wk, wv, gbias, head_g, skip, w_out, batch, seq):
    nt = seq // TILE
    T = batch * seq
    const2 = lambda b, i: (0, 0)
    const3 = lambda b, i: (0, 0, 0)

    def col_block(j):
        return pl.BlockSpec((TILE, D_MODEL), lambda b, i: (b * nt + i, j))

    return pl.pallas_call(
        _mlstm_kernel,
        grid=(batch, nt),
        in_specs=[
            col_block(0), col_block(1), col_block(2),
            pl.BlockSpec((GATE_ROWS, TILE), lambda b, i: (0, b * nt + i)),
            pl.BlockSpec((CONV_WIDTH, D_MODEL), const2),
            pl.BlockSpec((1, D_MODEL), const2),
            pl.BlockSpec((HEADS, HEAD_DIM, HEAD_DIM), const3),
            pl.BlockSpec((HEADS, HEAD_DIM, HEAD_DIM), const3),
            pl.BlockSpec((HEADS, HEAD_DIM, HEAD_DIM), const3),
            pl.BlockSpec((GATE_ROWS, 1), const2),
            pl.BlockSpec((1, D_MODEL), const2),
            pl.BlockSpec((1, D_MODEL), const2),
            pl.BlockSpec((D_MODEL, D_MODEL), const2),
        ],
        out_specs=pl.BlockSpec((TILE, D_MODEL), lambda b, i: (b * nt + i, 0)),
        out_shape=jax.ShapeDtypeStruct((T, D_MODEL), BF16),
        scratch_shapes=[
            pltpu.VMEM((TILE + 8, D_MODEL), F32),
            pltpu.VMEM((TILE, D_MODEL), F32),
            pltpu.VMEM((TILE, D_MODEL), BF16),
            pltpu.VMEM((TILE, D_MODEL), BF16),
            pltpu.VMEM((TILE, D_MODEL), BF16),
            pltpu.VMEM((TILE, D_MODEL), BF16),
            pltpu.VMEM((HEADS, HEAD_DIM, HEAD_DIM), F32),
            pltpu.VMEM((HEADS, 1, HEAD_DIM), F32),
            pltpu.VMEM((HEADS, 1, 1), F32),
        ],
        compiler_params=pltpu.CompilerParams(
            dimension_semantics=("arbitrary", "arbitrary"), vmem_limit_bytes=VMEM_LIMIT),
        name="mlstm",
    )(proj, proj, proj, gates, conv_w, conv_b, wq, wk, wv, gbias, head_g, skip, w_out)


S5_LANE_BLOCK = 512


def _s5_kernel(ub_ref, zb_ref, perm_ref, permt_ref, bre_ref, bim_ref, are_ref, aim_ref,
               apre_ref, apim_ref, cre_ref, cim_ref, dskip_ref, wglu_ref, bglu_ref, wout_ref,
               out_ref, sre_ref, sim_ref, xre_ref, xim_ref, y_ref):
    i = pl.program_id(1)

    @pl.when(i == 0)
    def _():
        xre_ref[...] = jnp.zeros_like(xre_ref)
        xim_ref[...] = jnp.zeros_like(xim_ref)

    u_perm = _dot(perm_ref[...], ub_ref[...]).astype(BF16)
    for kb in range(S5_WIDTH // 128):
        ukb = u_perm[:, kb * 128:(kb + 1) * 128]
        sre_ref[:, kb * 512:(kb + 1) * 512] = _dot(ukb, bre_ref[kb])
        sim_ref[:, kb * 512:(kb + 1) * 512] = _dot(ukb, bim_ref[kb])

    sub = lax.broadcasted_iota(jnp.int32, (SUBSEQ, S5_LANE_BLOCK), 0)
    for lb in range(S5_LANES // S5_LANE_BLOCK):
        lanes = slice(lb * S5_LANE_BLOCK, (lb + 1) * S5_LANE_BLOCK)
        ar = jnp.broadcast_to(are_ref[:, lanes], (SUBSEQ, S5_LANE_BLOCK))
        ai = jnp.broadcast_to(aim_ref[:, lanes], (SUBSEQ, S5_LANE_BLOCK))

        def end_step(j, carry):
            er, ei = carry
            r8 = pl.ds(pl.multiple_of(j * SUBSEQ, SUBSEQ), SUBSEQ)
            br = sre_ref[r8, lanes]
            bi = sim_ref[r8, lanes]
            return ar * er - ai * ei + br, ar * ei + ai * er + bi

        zero = jnp.zeros((SUBSEQ, S5_LANE_BLOCK), F32)
        er, ei = lax.fori_loop(0, SUBLEN, end_step, (zero, zero))

        pr = apre_ref[:, lanes]
        pi = apim_ref[:, lanes]
        xr = xre_ref[:, lanes]
        xi = xim_ref[:, lanes]
        start_r = zero
        start_i = zero
        for r in range(SUBSEQ):
            start_r = jnp.where(sub == r, xr, start_r)
            start_i = jnp.where(sub == r, xi, start_i)
            xr, xi = (pr * xr - pi * xi + er[r:r + 1, :], pr * xi + pi * xr + ei[r:r + 1, :])
        xre_ref[:, lanes] = xr
        xim_ref[:, lanes] = xi

        def scan_step(j, carry):
            sr, si = carry
            r8 = pl.ds(pl.multiple_of(j * SUBSEQ, SUBSEQ), SUBSEQ)
            br = sre_ref[r8, lanes]
            bi = sim_ref[r8, lanes]
            nr = ar * sr - ai * si + br
            ni = ar * si + ai * sr + bi
            sre_ref[r8, lanes] = nr
            sim_ref[r8, lanes] = ni
            return nr, ni

        lax.fori_loop(0, SUBLEN, scan_step, (start_r, start_i))

    permt = permt_ref[...]
    u = ub_ref[...].astype(F32)
    for ob in range(S5_WIDTH // 128):
        lanes = slice(ob * 512, (ob + 1) * 512)
        ocols = slice(ob * 128, (ob + 1) * 128)
        y_perm = (_dot(sre_ref[:, lanes].astype(BF16), cre_ref[ob])
                  + _dot(sim_ref[:, lanes].astype(BF16), cim_ref[ob]))
        y_hi = y_perm.astype(BF16)
        y_lo = (y_perm - y_hi.astype(F32)).astype(BF16)
        y = _dot(permt, y_hi) + _dot(permt, y_lo) + dskip_ref[:, ocols] * u[:, ocols]
        y = 0.5 * y * (1.0 + jnp.tanh(math.sqrt(2.0 / math.pi) * (y + 0.044715 * (y * y * y))))
        y_ref[:, ocols] = y
    y = y_ref[...]
    y = y * jax.nn.sigmoid(_dot(y.astype(BF16), wglu_ref[...]) + bglu_ref[...])
    y = y * _silu(zb_ref[...].astype(F32))
    out_ref[...] = _dot(y.astype(BF16), wout_ref[...]).astype(BF16)


def _s5(proj, perm, perm_t, b_re, b_im, a_re, a_im, ap_re, ap_im, c_re, c_im_neg, d_skip,
        w_glu, b_glu, w_out, batch, seq):
    nt = seq // TILE
    T = batch * seq
    const2 = lambda b, i: (0, 0)
    const3 = lambda b, i: (0, 0, 0)
    ub_block = 3072 // S5_WIDTH
    return pl.pallas_call(
        _s5_kernel,
        grid=(batch, nt),
        in_specs=[
            pl.BlockSpec((TILE, S5_WIDTH), lambda b, i: (b * nt + i, ub_block)),
            pl.BlockSpec((TILE, S5_WIDTH), lambda b, i: (b * nt + i, ub_block + 1)),
            pl.BlockSpec((TILE, TILE), const2),
            pl.BlockSpec((TILE, TILE), const2),
            pl.BlockSpec((4, 128, 512), const3),
            pl.BlockSpec((4, 128, 512), const3),
            pl.BlockSpec((1, S5_LANES), const2),
            pl.BlockSpec((1, S5_LANES), const2),
            pl.BlockSpec((1, S5_LANES), const2),
            pl.BlockSpec((1, S5_LANES), const2),
            pl.BlockSpec((4, 512, 128), const3),
            pl.BlockSpec((4, 512, 128), const3),
            pl.BlockSpec((1, S5_WIDTH), const2),
            pl.BlockSpec((S5_WIDTH, S5_WIDTH), const2),
            pl.BlockSpec((1, S5_WIDTH), const2),
            pl.BlockSpec((S5_WIDTH, D_MODEL), const2),
        ],
        out_specs=pl.BlockSpec((TILE, D_MODEL), lambda b, i: (b * nt + i, 0)),
        out_shape=jax.ShapeDtypeStruct((T, D_MODEL), BF16),
        scratch_shapes=[
            pltpu.VMEM((TILE, S5_LANES), F32),
            pltpu.VMEM((TILE, S5_LANES), F32),
            pltpu.VMEM((1, S5_LANES), F32),
            pltpu.VMEM((1, S5_LANES), F32),
            pltpu.VMEM((TILE, S5_WIDTH), F32),
        ],
        compiler_params=pltpu.CompilerParams(
            dimension_semantics=("arbitrary", "arbitrary"), vmem_limit_bytes=VMEM_LIMIT),
        name="s5",
    )(proj, proj, perm, perm_t, b_re, b_im, a_re, a_im, ap_re, ap_im, c_re, c_im_neg, d_skip,
      w_glu, b_glu, w_out)


def _merge_kernel(x_ref, gpre_ref, ya_ref, yb_ref, wo_ref, g_ref, out_ref):
    ga = jax.nn.sigmoid(gpre_ref[:, 0:D_MODEL].astype(F32))
    gb = jax.nn.sigmoid(gpre_ref[:, D_MODEL:2 * D_MODEL].astype(F32))
    merged = ga * ya_ref[...].astype(F32) + gb * yb_ref[...].astype(F32)
    y = _dot(merged.astype(BF16), wo_ref[...])
    y = y * lax.rsqrt(jnp.mean(y * y, axis=-1, keepdims=True) + NORM_EPS) * g_ref[...]
    out_ref[...] = x_ref[...] + y


def _merge(x2d, proj, y_a, y_b, w_o, norm_g):
    T = x2d.shape[0]
    const = lambda i: (0, 0)
    row = lambda i: (i, 0)
    return pl.pallas_call(
        _merge_kernel,
        grid=(T // TILE,),
        in_specs=[
            pl.BlockSpec((TILE, D_MODEL), row),
            pl.BlockSpec((TILE, 2 * D_MODEL), lambda i: (i, 4096 // (2 * D_MODEL))),
            pl.BlockSpec((TILE, D_MODEL), row),
            pl.BlockSpec((TILE, D_MODEL), row),
            pl.BlockSpec((D_MODEL, D_MODEL), const),
            pl.BlockSpec((1, D_MODEL), const),
        ],
        out_specs=pl.BlockSpec((TILE, D_MODEL), row),
        out_shape=jax.ShapeDtypeStruct((T, D_MODEL), F32),
        compiler_params=pltpu.CompilerParams(
            dimension_semantics=("arbitrary",), vmem_limit_bytes=VMEM_LIMIT),
        name="merge",
    )(x2d, proj, y_a, y_b, w_o, norm_g)


def _s5_params(lam_re, lam_im, log_dt, B_re, B_im, C_re, C_im):
    G, P, N = S5_GROUPS, S5_STATE, S5_GROUP
    dt = jnp.exp(log_dt.astype(F32))[:, None]
    lr = lam_re.astype(F32)
    li = lam_im.astype(F32)
    mag = jnp.exp(lr * dt)
    a_re = mag * jnp.cos(li * dt)
    a_im = mag * jnp.sin(li * dt)
    den = lr * lr + li * li
    nr = a_re - 1.0
    q_re = (nr * lr + a_im * li) / den
    q_im = (a_im * lr - nr * li) / den
    Br = B_re.astype(F32)
    Bi = B_im.astype(F32)
    bb_re = q_re[..., None] * Br - q_im[..., None] * Bi
    bb_im = q_re[..., None] * Bi + q_im[..., None] * Br

    gb = 128 // N
    eye_g = jnp.eye(gb, dtype=F32)

    def in_blocks(bb):
        return jnp.einsum('kgpn,gh->kgnhp', bb.reshape(G // gb, gb, P, N),
                          eye_g).reshape(G // gb, gb * N, gb * P).astype(BF16)

    def out_blocks(cc):
        return jnp.einsum('kgnp,gh->kgphn', cc.reshape(G // gb, gb, N, P),
                          eye_g).reshape(G // gb, gb * P, gb * N).astype(BF16)

    pr, pi = a_re, a_im
    for _ in range(int(math.log2(SUBLEN))):
        pr, pi = pr * pr - pi * pi, 2.0 * pr * pi
    flat = lambda a: a.reshape(1, G * P)
    return (in_blocks(bb_re), in_blocks(bb_im), flat(a_re), flat(a_im), flat(pr), flat(pi),
            out_blocks(C_re.astype(F32)), out_blocks(-C_im.astype(F32)))


def kernel(x, norm_pre_g, w_in, conv_w, conv_b, w_q, w_k, w_v, b_i, b_f, head_g, skip_a,
           w_a_out, lam_re, lam_im, log_dt, B_re, B_im, C_re, C_im, D_skip, w_glu, b_glu,
           w_b_out, w_o, norm_post_g):
    batch, seq, _ = x.shape
    depth = w_in.shape[0]
    T = batch * seq
    h = x.reshape(T, D_MODEL)

    p = np.arange(TILE)
    src = (p % SUBSEQ) * SUBLEN + p // SUBSEQ
    perm_np = np.zeros((TILE, TILE), np.float32)
    perm_np[p, src] = 1.0
    perm = jnp.asarray(perm_np, BF16)
    perm_t = jnp.asarray(perm_np.T, BF16)

    for l in range(depth):
        w = w_in[l]
        w_main = jnp.concatenate([w[:, 0:3072], w[:, 3080:3080 + 3072]], axis=1).astype(BF16)
        w_gate_t = jnp.zeros((GATE_ROWS, D_MODEL), F32).at[0:8].set(w[:, 3072:3080].T).astype(BF16)
        gbias = jnp.zeros((GATE_ROWS, 1), F32).at[0:4, 0].set(b_i[l]).at[4:8, 0].set(b_f[l])

        proj, gates = _in_proj(h, norm_pre_g[l][None, :], w_main, w_gate_t)

        y_a = _mlstm(proj, gates, conv_w[l], conv_b[l][None, :], w_q[l].astype(BF16),
                     (w_k[l] * (HEAD_DIM ** -0.5)).astype(BF16), w_v[l].astype(BF16), gbias,
                     head_g[l][None, :], skip_a[l][None, :], w_a_out[l].astype(BF16), batch, seq)

        bin_re, bin_im, a_re, a_im, ap_re, ap_im, cout_re, cout_im = _s5_params(
            lam_re[l], lam_im[l], log_dt[l], B_re[l], B_im[l], C_re[l], C_im[l])
        y_b = _s5(proj, perm, perm_t, bin_re, bin_im, a_re, a_im, ap_re, ap_im, cout_re, cout_im,
                  D_skip[l].reshape(1, S5_WIDTH), w_glu[l].astype(BF16), b_glu[l][None, :],
                  w_b_out[l].astype(BF16), batch, seq)

        h = _merge(h, proj, y_a, y_b, w_o[l].astype(BF16), norm_post_g[l][None, :])
    return h.reshape(batch, seq, D_MODEL)
```

```python
import math

import jax
import jax.numpy as jnp
import numpy as np
from jax import lax
from jax.experimental import pallas as pl
from jax.experimental.pallas import tpu as pltpu

F32 = jnp.float32
BF16 = jnp.bfloat16

D_MODEL = 1024
HEADS = 4
HEAD_DIM = 256
CHUNK = 128
CONV_WIDTH = 4
S5_WIDTH = 512
S5_GROUP = 16
S5_GROUPS = 32
S5_STATE = 64
S5_LANES = S5_GROUPS * S5_STATE
NORM_EPS = 1e-6
GATE_ROWS = 16

COL_UA, COL_ZA, COL_OA, COL_UB, COL_ZB, COL_G = 0, 1024, 2048, 3072, 3584, 4096
MAIN_COLS = 6144
PROJ_BLOCK = 1024

TILE = 256
SUBSEQ = 8
SUBLEN = TILE // SUBSEQ
S5_LANE_BLOCK = 512
VMEM_LIMIT = 60 * 1024 * 1024


def _dot(a, b):
    return jnp.dot(a, b, preferred_element_type=F32)


def _dot_nt(a, b):
    return lax.dot_general(a, b, (((1,), (1,)), ((), ())), preferred_element_type=F32)


def _dot_tn(a, b):
    return lax.dot_general(a, b, (((0,), (0,)), ((), ())), preferred_element_type=F32)


def _silu(x):
    return x * jax.nn.sigmoid(x)


def _rms(x):
    return x * lax.rsqrt(jnp.mean(x * x, axis=-1, keepdims=True) + NORM_EPS)


def _lane_cumsum(x):
    lane = lax.broadcasted_iota(jnp.int32, x.shape, 1)
    d = 1
    while d < x.shape[1]:
        x = x + jnp.where(lane >= d, pltpu.roll(x, d, axis=1), 0.0)
        d *= 2
    return x


class _Proj:
    def __init__(self, x_ref, g_ref, w_ref, wgt_ref, proj_ref, gates_ref, slot):
        self.xn = (_rms(x_ref[...]) * g_ref[...]).astype(BF16)
        self.w_ref, self.wgt_ref = w_ref, wgt_ref
        self.proj_ref, self.gates_ref, self.slot = proj_ref, gates_ref, slot

    def block(self, j):
        cols = slice(j * PROJ_BLOCK, (j + 1) * PROJ_BLOCK)
        self.proj_ref[self.slot, :, cols] = _dot(self.xn, self.w_ref[:, cols]).astype(BF16)

    def gates(self):
        self.gates_ref[self.slot] = _dot_nt(self.wgt_ref[...], self.xn)


def _mlstm_prepare(proj_ref, slot, convw_ref, convb_ref, wq_ref, wk_ref, wv_ref,
                   ext_ref, c_ref, q_ref, k_ref, v_ref):
    ext_ref[8:8 + TILE, :] = proj_ref[slot, :, COL_UA:COL_UA + D_MODEL].astype(F32)
    for h in range(HEADS):
        cols = slice(h * HEAD_DIM, (h + 1) * HEAD_DIM)
        acc = convb_ref[:, cols]
        for j in range(CONV_WIDTH):
            shift = CONV_WIDTH - 1 - j
            acc = acc + convw_ref[j:j + 1, cols] * ext_ref[8 - shift:8 - shift + TILE, cols]
        c = _silu(acc)
        c_ref[:, cols] = c
        cb = c.astype(BF16)
        q_ref[:, cols] = _dot(cb, wq_ref[h]).astype(BF16)
        k_ref[:, cols] = _dot(cb, wk_ref[h]).astype(BF16)
        ua = proj_ref[slot, :, COL_UA + h * HEAD_DIM:COL_UA + (h + 1) * HEAD_DIM]
        v_ref[:, cols] = _dot(ua, wv_ref[h]).astype(BF16)
    ext_ref[0:8, :] = ext_ref[TILE:TILE + 8, :]


def _mlstm_chunk(c_idx, proj_ref, gates_ref, slot, gbias_ref, headg_ref, skip_ref,
                 c_ref, q_ref, k_ref, v_ref, hfin_ref, ct_ref, n_ref, m_ref):
    row = lax.broadcasted_iota(jnp.int32, (CHUNK, CHUNK), 0)
    col = lax.broadcasted_iota(jnp.int32, (CHUNK, CHUNK), 1)
    eye = row == col
    causal = row >= col
    grow = lax.broadcasted_iota(jnp.int32, (GATE_ROWS, CHUNK), 0)
    glane = lax.broadcasted_iota(jnp.int32, (1, CHUNK), 1)

    rows = slice(c_idx * CHUNK, (c_idx + 1) * CHUNK)
    pre = gates_ref[slot, :, rows] + gbias_ref[...]
    logsig = jnp.minimum(pre, 0.0) - jnp.log1p(jnp.exp(-jnp.abs(pre)))
    gate = jnp.where(grow >= HEADS, logsig, pre)
    bsum = _lane_cumsum(gate)
    for h in range(HEADS):
        cols = slice(h * HEAD_DIM, (h + 1) * HEAD_DIM)
        ig_r = gate[h:h + 1, :]
        b_r = bsum[HEADS + h:HEADS + h + 1, :]
        b_c = jnp.sum(jnp.where(eye, b_r, 0.0), axis=-1, keepdims=True)
        ig_c = jnp.sum(jnp.where(eye, ig_r, 0.0), axis=-1, keepdims=True)
        b_last = jnp.sum(jnp.where(glane == CHUNK - 1, b_r, 0.0), axis=-1, keepdims=True)
        m_prev = m_ref[h]
        q = q_ref[rows, cols]
        k = k_ref[rows, cols]
        v = v_ref[rows, cols]

        log_d = jnp.where(causal, b_c - b_r + ig_r, -jnp.inf)
        log_inter = b_c + m_prev
        m_t = jnp.maximum(log_inter, jnp.max(log_d, axis=-1, keepdims=True))
        dmat = jnp.exp(log_d - m_t)
        inter = jnp.exp(log_inter - m_t)
        s = _dot_nt(q, k) * dmat
        ct_prev = ct_ref[h]
        num = _dot(s.astype(BF16), v) + inter * _dot(q, ct_prev.astype(BF16))
        qn = jnp.sum(q.astype(F32) * n_ref[h], axis=-1, keepdims=True)
        den = jnp.sum(s, axis=-1, keepdims=True) + inter * qn
        hh = num * (1.0 / jnp.maximum(jnp.abs(den), jnp.exp(-m_t)))

        w_end_r = b_last - b_r + ig_r
        m_loc = jnp.max(w_end_r, axis=-1, keepdims=True)
        m_new = jnp.maximum(b_last + m_prev, m_loc)
        decay = jnp.exp(b_last + m_prev - m_new)
        e_c = jnp.exp(b_last - b_c + ig_c - m_new)
        kw = k.astype(F32) * e_c
        ct_ref[h] = decay * ct_prev + _dot_tn(kw.astype(BF16), v)
        n_ref[h] = decay * n_ref[h] + jnp.sum(kw, axis=0, keepdims=True)
        m_ref[h] = m_new

        o_pre = proj_ref[slot, rows, COL_OA + h * HEAD_DIM:COL_OA + (h + 1) * HEAD_DIM]
        z_pre = proj_ref[slot, rows, COL_ZA + h * HEAD_DIM:COL_ZA + (h + 1) * HEAD_DIM]
        hh = _rms(jax.nn.sigmoid(o_pre.astype(F32)) * hh)
        hh = hh * headg_ref[:, cols] + skip_ref[:, cols] * c_ref[rows, cols]
        hh = hh * _silu(z_pre.astype(F32))
        hfin_ref[rows, cols] = hh.astype(BF16)


def _s5_inputs(proj_ref, slot, perm_ref, bre_ref, bim_ref, sre_ref, sim_ref):
    u_perm = _dot(perm_ref[...], proj_ref[slot, :, COL_UB:COL_UB + S5_WIDTH]).astype(BF16)
    for kb in range(S5_WIDTH // 128):
        ukb = u_perm[:, kb * 128:(kb + 1) * 128]
        sre_ref[:, kb * 512:(kb + 1) * 512] = _dot(ukb, bre_ref[kb])
        sim_ref[:, kb * 512:(kb + 1) * 512] = _dot(ukb, bim_ref[kb])


def _s5_scan(lb, are_ref, aim_ref, apre_ref, apim_ref, sre_ref, sim_ref, xre_ref, xim_ref):
    lanes = slice(lb * S5_LANE_BLOCK, (lb + 1) * S5_LANE_BLOCK)
    shape = (SUBSEQ, S5_LANE_BLOCK)
    ar = jnp.broadcast_to(are_ref[:, lanes], shape)
    ai = jnp.broadcast_to(aim_ref[:, lanes], shape)

    def step(j, sr, si):
        r8 = slice(j * SUBSEQ, (j + 1) * SUBSEQ)
        br = sre_ref[r8, lanes]
        bi = sim_ref[r8, lanes]
        return ar * sr - ai * si + br, ar * si + ai * sr + bi

    er = jnp.zeros(shape, F32)
    ei = jnp.zeros(shape, F32)
    for j in range(SUBLEN):
        er, ei = step(j, er, ei)

    sub = lax.broadcasted_iota(jnp.int32, shape, 0)
    pr = apre_ref[:, lanes]
    pi = apim_ref[:, lanes]
    xr = xre_ref[:, lanes]
    xi = xim_ref[:, lanes]
    sr = jnp.zeros(shape, F32)
    si = jnp.zeros(shape, F32)
    for r in range(SUBSEQ):
        sr = jnp.where(sub == r, xr, sr)
        si = jnp.where(sub == r, xi, si)
        xr, xi = (pr * xr - pi * xi + er[r:r + 1, :], pr * xi + pi * xr + ei[r:r + 1, :])
    xre_ref[:, lanes] = xr
    xim_ref[:, lanes] = xi

    for j in range(SUBLEN):
        sr, si = step(j, sr, si)
        r8 = slice(j * SUBSEQ, (j + 1) * SUBSEQ)
        sre_ref[r8, lanes] = sr
        sim_ref[r8, lanes] = si


def _s5_outputs(proj_ref, slot, permt_ref, cre_ref, cim_ref, dskip_ref, wglu_ref, bglu_ref,
                wbout_ref, sre_ref, sim_ref, yperm_ref):
    for ob in range(S5_WIDTH // 128):
        lanes = slice(ob * 512, (ob + 1) * 512)
        yperm_ref[:, ob * 128:(ob + 1) * 128] = (
            _dot(sre_ref[:, lanes].astype(BF16), cre_ref[ob])
            + _dot(sim_ref[:, lanes].astype(BF16), cim_ref[ob]))
    y_perm = yperm_ref[...]
    y_hi = y_perm.astype(BF16)
    y_lo = (y_perm - y_hi.astype(F32)).astype(BF16)
    permt = permt_ref[...]
    u = proj_ref[slot, :, COL_UB:COL_UB + S5_WIDTH].astype(F32)
    y = _dot(permt, y_hi) + _dot(permt, y_lo) + dskip_ref[...] * u
    y = 0.5 * y * (1.0 + jnp.tanh(math.sqrt(2.0 / math.pi) * (y + 0.044715 * (y * y * y))))
    y = y * jax.nn.sigmoid(_dot(y.astype(BF16), wglu_ref[...]) + bglu_ref[...])
    y = y * _silu(proj_ref[slot, :, COL_ZB:COL_ZB + S5_WIDTH].astype(F32))
    return _dot(y.astype(BF16), wbout_ref[...])


def _block_kernel(nt,
                  xcur_ref, xnxt_ref, gpre_ref, w_ref, wgt_ref,
                  convw_ref, convb_ref, wq_ref, wk_ref, wv_ref, gbias_ref, headg_ref, skip_ref,
                  waout_ref,
                  perm_ref, permt_ref, bre_ref, bim_ref, are_ref, aim_ref, apre_ref, apim_ref,
                  cre_ref, cim_ref, dskip_ref, wglu_ref, bglu_ref, wbout_ref,
                  wo_ref, gpost_ref,
                  out_ref,
                  proj_ref, gates_ref,
                  ext_ref, c_ref, q_ref, k_ref, v_ref, hfin_ref, ct_ref, n_ref, m_ref,
                  sre_ref, sim_ref, xre_ref, xim_ref, yperm_ref):
    step = pl.program_id(0)
    slot = step % 2
    nslot = 1 - slot

    @pl.when(step == 0)
    def _():
        first = _Proj(xcur_ref, gpre_ref, w_ref, wgt_ref, proj_ref, gates_ref, 0)
        for j in range(MAIN_COLS // PROJ_BLOCK):
            first.block(j)
        first.gates()

    @pl.when(step % nt == 0)
    def _():
        ext_ref[0:8, :] = jnp.zeros((8, D_MODEL), F32)
        ct_ref[...] = jnp.zeros_like(ct_ref)
        n_ref[...] = jnp.zeros_like(n_ref)
        m_ref[...] = jnp.full(m_ref.shape, -jnp.inf, F32)
        xre_ref[...] = jnp.zeros_like(xre_ref)
        xim_ref[...] = jnp.zeros_like(xim_ref)

    nxt = _Proj(xnxt_ref, gpre_ref, w_ref, wgt_ref, proj_ref, gates_ref, nslot)

    nxt.block(0)
    _mlstm_prepare(proj_ref, slot, convw_ref, convb_ref, wq_ref, wk_ref, wv_ref,
                   ext_ref, c_ref, q_ref, k_ref, v_ref)
    nxt.block(1)
    _s5_inputs(proj_ref, slot, perm_ref, bre_ref, bim_ref, sre_ref, sim_ref)
    nxt.block(2)
    for lb in range(S5_LANES // S5_LANE_BLOCK):
        _s5_scan(lb, are_ref, aim_ref, apre_ref, apim_ref, sre_ref, sim_ref, xre_ref, xim_ref)
    nxt.block(3)
    for c_idx in range(TILE // CHUNK):
        _mlstm_chunk(c_idx, proj_ref, gates_ref, slot, gbias_ref, headg_ref, skip_ref,
                     c_ref, q_ref, k_ref, v_ref, hfin_ref, ct_ref, n_ref, m_ref)
        nxt.block(4 + c_idx)
    nxt.gates()
    y_a = _dot(hfin_ref[...], waout_ref[...])
    y_b = _s5_outputs(proj_ref, slot, permt_ref, cre_ref, cim_ref, dskip_ref, wglu_ref, bglu_ref,
                      wbout_ref, sre_ref, sim_ref, yperm_ref)

    ga = jax.nn.sigmoid(proj_ref[slot, :, COL_G:COL_G + D_MODEL].astype(F32))
    gb = jax.nn.sigmoid(proj_ref[slot, :, COL_G + D_MODEL:COL_G + 2 * D_MODEL].astype(F32))
    merged = (ga * y_a + gb * y_b).astype(BF16)
    out_ref[...] = xcur_ref[...] + _rms(_dot(merged, wo_ref[...])) * gpost_ref[...]


def _block(x2d, nt, weights):
    T = x2d.shape[0]
    nsteps = T // TILE

    def resident(a):
        nd = a.ndim
        return pl.BlockSpec(a.shape, lambda s: (0,) * nd, pipeline_mode=pl.Buffered(1))

    return pl.pallas_call(
        lambda *refs: _block_kernel(nt, *refs),
        grid=(nsteps,),
        in_specs=[
            pl.BlockSpec((TILE, D_MODEL), lambda s: (s, 0)),
            pl.BlockSpec((TILE, D_MODEL), lambda s: (jnp.minimum(s + 1, nsteps - 1), 0)),
        ] + [resident(w) for w in weights],
        out_specs=pl.BlockSpec((TILE, D_MODEL), lambda s: (s, 0)),
        out_shape=jax.ShapeDtypeStruct((T, D_MODEL), F32),
        scratch_shapes=[
            pltpu.VMEM((2, TILE, MAIN_COLS), BF16),
            pltpu.VMEM((2, GATE_ROWS, TILE), F32),
            pltpu.VMEM((TILE + 8, D_MODEL), F32),
            pltpu.VMEM((TILE, D_MODEL), F32),
            pltpu.VMEM((TILE, D_MODEL), BF16),
            pltpu.VMEM((TILE, D_MODEL), BF16),
            pltpu.VMEM((TILE, D_MODEL), BF16),
            pltpu.VMEM((TILE, D_MODEL), BF16),
            pltpu.VMEM((HEADS, HEAD_DIM, HEAD_DIM), F32),
            pltpu.VMEM((HEADS, 1, HEAD_DIM), F32),
            pltpu.VMEM((HEADS, 1, 1), F32),
            pltpu.VMEM((TILE, S5_LANES), F32),
            pltpu.VMEM((TILE, S5_LANES), F32),
            pltpu.VMEM((1, S5_LANES), F32),
            pltpu.VMEM((1, S5_LANES), F32),
            pltpu.VMEM((TILE, S5_WIDTH), F32),
        ],
        compiler_params=pltpu.CompilerParams(
            dimension_semantics=("arbitrary",), vmem_limit_bytes=VMEM_LIMIT),
        name="hybrid_block",
    )(x2d, x2d, *weights)


def _s5_params(lam_re, lam_im, log_dt, B_re, B_im, C_re, C_im):
    G, P, N = S5_GROUPS, S5_STATE, S5_GROUP
    dt = jnp.exp(log_dt.astype(F32))[:, None]
    lr = lam_re.astype(F32)
    li = lam_im.astype(F32)
    mag = jnp.exp(lr * dt)
    a_re = mag * jnp.cos(li * dt)
    a_im = mag * jnp.sin(li * dt)
    den = lr * lr + li * li
    nr = a_re - 1.0
    q_re = (nr * lr + a_im * li) / den
    q_im = (a_im * lr - nr * li) / den
    Br = B_re.astype(F32)
    Bi = B_im.astype(F32)
    bb_re = q_re[..., None] * Br - q_im[..., None] * Bi
    bb_im = q_re[..., None] * Bi + q_im[..., None] * Br

    gb = 128 // N
    eye_g = jnp.eye(gb, dtype=F32)

    def in_blocks(bb):
        return jnp.einsum('kgpn,gh->kgnhp', bb.reshape(G // gb, gb, P, N),
                          eye_g).reshape(G // gb, gb * N, gb * P).astype(BF16)

    def out_blocks(cc):
        return jnp.einsum('kgnp,gh->kgphn', cc.reshape(G // gb, gb, N, P),
                          eye_g).reshape(G // gb, gb * P, gb * N).astype(BF16)

    pr, pi = a_re, a_im
    for _ in range(int(math.log2(SUBLEN))):
        pr, pi = pr * pr - pi * pi, 2.0 * pr * pi
    flat = lambda a: a.reshape(1, G * P)
    return (in_blocks(bb_re), in_blocks(bb_im), flat(a_re), flat(a_im), flat(pr), flat(pi),
            out_blocks(C_re.astype(F32)), out_blocks(-C_im.astype(F32)))


def kernel(x, norm_pre_g, w_in, conv_w, conv_b, w_q, w_k, w_v, b_i, b_f, head_g, skip_a,
           w_a_out, lam_re, lam_im, log_dt, B_re, B_im, C_re, C_im, D_skip, w_glu, b_glu,
           w_b_out, w_o, norm_post_g):
    batch, seq, _ = x.shape
    depth = w_in.shape[0]
    T = batch * seq
    h = x.reshape(T, D_MODEL)

    p = np.arange(TILE)
    src = (p % SUBSEQ) * SUBLEN + p // SUBSEQ
    perm_np = np.zeros((TILE, TILE), np.float32)
    perm_np[p, src] = 1.0
    perm = jnp.asarray(perm_np, BF16)
    perm_t = jnp.asarray(perm_np.T, BF16)

    for l in range(depth):
        w = w_in[l]
        w_main = jnp.concatenate([w[:, 0:3072], w[:, 3080:3080 + 3072]], axis=1).astype(BF16)
        w_gate_t = jnp.zeros((GATE_ROWS, D_MODEL), F32).at[0:8].set(w[:, 3072:3080].T).astype(BF16)
        gbias = jnp.zeros((GATE_ROWS, 1), F32).at[0:4, 0].set(b_i[l]).at[4:8, 0].set(b_f[l])
        bin_re, bin_im, a_re, a_im, ap_re, ap_im, cout_re, cout_im = _s5_params(
            lam_re[l], lam_im[l], log_dt[l], B_re[l], B_im[l], C_re[l], C_im[l])
        weights = (
            norm_pre_g[l][None, :], w_main, w_gate_t,
            conv_w[l], conv_b[l][None, :], w_q[l].astype(BF16),
            (w_k[l] * (HEAD_DIM ** -0.5)).astype(BF16), w_v[l].astype(BF16), gbias,
            head_g[l][None, :], skip_a[l][None, :], w_a_out[l].astype(BF16),
            perm, perm_t, bin_re, bin_im, a_re, a_im, ap_re, ap_im, cout_re, cout_im,
            D_skip[l].reshape(1, S5_WIDTH), w_glu[l].astype(BF16), b_glu[l][None, :],
            w_b_out[l].astype(BF16),
            w_o[l].astype(BF16), norm_post_g[l][None, :],
        )
        h = _block(h, seq // TILE, weights)
    return h.reshape(batch, seq, D_MODEL)
```

```python
import functools
import math

import jax
import jax.numpy as jnp
import numpy as np
from jax import lax
from jax.experimental import pallas as pl
from jax.experimental.pallas import tpu as pltpu

F32 = jnp.float32
BF16 = jnp.bfloat16

D_MODEL = 1024
HEADS = 4
HEAD_DIM = 256
CHUNK = 128
CONV_WIDTH = 4
S5_WIDTH = 512
S5_GROUP = 16
S5_GROUPS = 32
S5_STATE = 64
S5_LANES = S5_GROUPS * S5_STATE
NORM_EPS = 1e-6
GATE_ROWS = 16

COL_UA, COL_ZA, COL_OA, COL_UB, COL_ZB, COL_G = 0, 1024, 2048, 3072, 3584, 4096
MAIN_COLS = 6144
PROJ_BLOCK = 1024

TILE = 256
SUBSEQ = 8
SUBLEN = TILE // SUBSEQ
S5_LANE_BLOCK = 512
VMEM_LIMIT = 60 * 1024 * 1024


def _dot(a, b):
    return jnp.dot(a, b, preferred_element_type=F32)


def _dot_nt(a, b):
    return lax.dot_general(a, b, (((1,), (1,)), ((), ())), preferred_element_type=F32)


def _dot_tn(a, b):
    return lax.dot_general(a, b, (((0,), (0,)), ((), ())), preferred_element_type=F32)


def _to_bf16(w):
    return w.astype(BF16)


def _silu(x):
    return x * jax.nn.sigmoid(x)


def _rms(x):
    return x * lax.rsqrt(jnp.mean(x * x, axis=-1, keepdims=True) + NORM_EPS)


def _lane_cumsum(x):
    lane = lax.broadcasted_iota(jnp.int32, x.shape, 1)
    d = 1
    while d < x.shape[1]:
        x = x + jnp.where(lane >= d, pltpu.roll(x, d, axis=1), 0.0)
        d *= 2
    return x


class _Proj:
    def __init__(self, x_ref, g_ref, w_ref, wgt_ref, xn_ref, proj_ref, gates_ref, slot):
        xn_ref[...] = (_rms(x_ref[...]) * g_ref[...]).astype(BF16)
        self.w_ref, self.wgt_ref, self.xn_ref = w_ref, wgt_ref, xn_ref
        self.proj_ref, self.gates_ref, self.slot = proj_ref, gates_ref, slot

    def block(self, j):
        cols = slice(j * PROJ_BLOCK, (j + 1) * PROJ_BLOCK)
        self.proj_ref[self.slot, :, cols] = _dot(
            self.xn_ref[...], self.w_ref[:, cols]).astype(BF16)

    def gates(self):
        self.gates_ref[self.slot] = _dot_nt(self.wgt_ref[...], self.xn_ref[...])


def _mlstm_conv(proj_ref, slot, seq_start, convw_ref, convb_ref, ext_ref, c_ref, cb_ref):
    ext_ref[0:8, :] = jnp.where(seq_start, 0.0, ext_ref[TILE:TILE + 8, :])
    ext_ref[8:8 + TILE, :] = proj_ref[slot, :, COL_UA:COL_UA + D_MODEL].astype(F32)
    for h in range(HEADS):
        cols = slice(h * HEAD_DIM, (h + 1) * HEAD_DIM)
        acc = convb_ref[:, cols]
        for j in range(CONV_WIDTH):
            shift = CONV_WIDTH - 1 - j
            acc = acc + convw_ref[j:j + 1, cols] * ext_ref[8 - shift:8 - shift + TILE, cols]
        c = _silu(acc)
        c_ref[:, cols] = c
        cb_ref[:, cols] = c.astype(BF16)


def _mlstm_qkv(proj_ref, slot, wq_ref, wk_ref, wv_ref, cb_ref, q_ref, k_ref, v_ref):
    for h in range(HEADS):
        cols = slice(h * HEAD_DIM, (h + 1) * HEAD_DIM)
        cb = cb_ref[:, cols]
        q_ref[:, cols] = _dot(cb, wq_ref[h]).astype(BF16)
        k_ref[:, cols] = _dot(cb, wk_ref[h]).astype(BF16)
        ua = proj_ref[slot, :, COL_UA + h * HEAD_DIM:COL_UA + (h + 1) * HEAD_DIM]
        v_ref[:, cols] = _dot(ua, wv_ref[h]).astype(BF16)


def _mlstm_chunk(c_idx, proj_ref, gates_ref, slot, gbias_ref, headg_ref, skip_ref,
                 c_ref, q_ref, k_ref, v_ref, hfin_ref, ct_ref, n_ref, m_ref):
    row = lax.broadcasted_iota(jnp.int32, (CHUNK, CHUNK), 0)
    col = lax.broadcasted_iota(jnp.int32, (CHUNK, CHUNK), 1)
    eye = row == col
    causal = row >= col
    grow = lax.broadcasted_iota(jnp.int32, (GATE_ROWS, CHUNK), 0)
    glane = lax.broadcasted_iota(jnp.int32, (1, CHUNK), 1)

    rows = slice(c_idx * CHUNK, (c_idx + 1) * CHUNK)
    pre = gates_ref[slot, :, rows] + gbias_ref[...]
    logsig = jnp.minimum(pre, 0.0) - jnp.log1p(jnp.exp(-jnp.abs(pre)))
    gate = jnp.where(grow >= HEADS, logsig, pre)
    bsum = _lane_cumsum(gate)
    for h in range(HEADS):
        cols = slice(h * HEAD_DIM, (h + 1) * HEAD_DIM)
        ig_r = gate[h:h + 1, :]
        b_r = bsum[HEADS + h:HEADS + h + 1, :]
        b_c = jnp.sum(jnp.where(eye, b_r, 0.0), axis=-1, keepdims=True)
        ig_c = jnp.sum(jnp.where(eye, ig_r, 0.0), axis=-1, keepdims=True)
        b_last = jnp.sum(jnp.where(glane == CHUNK - 1, b_r, 0.0), axis=-1, keepdims=True)
        m_prev = m_ref[h]
        q = q_ref[rows, cols]
        k = k_ref[rows, cols]
        v = v_ref[rows, cols]

        log_d = jnp.where(causal, b_c - b_r + ig_r, -jnp.inf)
        log_inter = b_c + m_prev
        m_t = jnp.maximum(log_inter, jnp.max(log_d, axis=-1, keepdims=True))
        dmat = jnp.exp(log_d - m_t)
        inter = jnp.exp(log_inter - m_t)
        s = _dot_nt(q, k) * dmat
        ct_prev = ct_ref[h]
        num = _dot(s.astype(BF16), v) + inter * _dot(q, ct_prev.astype(BF16))
        qn = jnp.sum(q.astype(F32) * n_ref[h], axis=-1, keepdims=True)
        den = jnp.sum(s, axis=-1, keepdims=True) + inter * qn
        hh = num * (1.0 / jnp.maximum(jnp.abs(den), jnp.exp(-m_t)))

        w_end_r = b_last - b_r + ig_r
        m_loc = jnp.max(w_end_r, axis=-1, keepdims=True)
        m_new = jnp.maximum(b_last + m_prev, m_loc)
        decay = jnp.exp(b_last + m_prev - m_new)
        e_c = jnp.exp(b_last - b_c + ig_c - m_new)
        kw = k.astype(F32) * e_c
        ct_ref[h] = decay * ct_prev + _dot_tn(kw.astype(BF16), v)
        n_ref[h] = decay * n_ref[h] + jnp.sum(kw, axis=0, keepdims=True)
        m_ref[h] = m_new

        o_pre = proj_ref[slot, rows, COL_OA + h * HEAD_DIM:COL_OA + (h + 1) * HEAD_DIM]
        z_pre = proj_ref[slot, rows, COL_ZA + h * HEAD_DIM:COL_ZA + (h + 1) * HEAD_DIM]
        hh = _rms(jax.nn.sigmoid(o_pre.astype(F32)) * hh)
        hh = hh * headg_ref[:, cols] + skip_ref[:, cols] * c_ref[rows, cols]
        hh = hh * _silu(z_pre.astype(F32))
        hfin_ref[rows, cols] = hh.astype(BF16)


def _s5_inputs(proj_ref, slot, perm_ref, bre_ref, bim_ref, sre_ref, sim_ref):
    u_perm = _dot(perm_ref[...], proj_ref[slot, :, COL_UB:COL_UB + S5_WIDTH]).astype(BF16)
    for kb in range(S5_WIDTH // 128):
        ukb = u_perm[:, kb * 128:(kb + 1) * 128]
        sre_ref[:, kb * 512:(kb + 1) * 512] = _dot(ukb, bre_ref[kb])
        sim_ref[:, kb * 512:(kb + 1) * 512] = _dot(ukb, bim_ref[kb])


def _s5_scan(lb, are_ref, aim_ref, apre_ref, apim_ref, sre_ref, sim_ref, xre_ref, xim_ref):
    lanes = slice(lb * S5_LANE_BLOCK, (lb + 1) * S5_LANE_BLOCK)
    shape = (SUBSEQ, S5_LANE_BLOCK)
    ar = jnp.broadcast_to(are_ref[:, lanes], shape)
    ai = jnp.broadcast_to(aim_ref[:, lanes], shape)

    def step(j, sr, si):
        r8 = slice(j * SUBSEQ, (j + 1) * SUBSEQ)
        br = sre_ref[r8, lanes]
        bi = sim_ref[r8, lanes]
        return ar * sr - ai * si + br, ar * si + ai * sr + bi

    er = jnp.zeros(shape, F32)
    ei = jnp.zeros(shape, F32)
    for j in range(SUBLEN):
        er, ei = step(j, er, ei)

    sub = lax.broadcasted_iota(jnp.int32, shape, 0)
    pr = apre_ref[:, lanes]
    pi = apim_ref[:, lanes]
    xr = xre_ref[:, lanes]
    xi = xim_ref[:, lanes]
    sr = jnp.zeros(shape, F32)
    si = jnp.zeros(shape, F32)
    for r in range(SUBSEQ):
        sr = jnp.where(sub == r, xr, sr)
        si = jnp.where(sub == r, xi, si)
        xr, xi = (pr * xr - pi * xi + er[r:r + 1, :], pr * xi + pi * xr + ei[r:r + 1, :])
    xre_ref[:, lanes] = xr
    xim_ref[:, lanes] = xi

    for j in range(SUBLEN):
        sr, si = step(j, sr, si)
        r8 = slice(j * SUBSEQ, (j + 1) * SUBSEQ)
        sre_ref[r8, lanes] = sr
        sim_ref[r8, lanes] = si


def _s5_outputs(proj_ref, slot, permt_ref, cre_ref, cim_ref, dskip_ref, wglu_ref, bglu_ref,
                wbout_ref, sre_ref, sim_ref, yperm_ref):
    for ob in range(S5_WIDTH // 128):
        lanes = slice(ob * 512, (ob + 1) * 512)
        yperm_ref[:, ob * 128:(ob + 1) * 128] = (
            _dot(sre_ref[:, lanes].astype(BF16), cre_ref[ob])
            + _dot(sim_ref[:, lanes].astype(BF16), cim_ref[ob])).astype(BF16)
    u = proj_ref[slot, :, COL_UB:COL_UB + S5_WIDTH].astype(F32)
    y = _dot(permt_ref[...], yperm_ref[...]) + dskip_ref[...] * u
    y = 0.5 * y * (1.0 + jnp.tanh(math.sqrt(2.0 / math.pi) * (y + 0.044715 * (y * y * y))))
    y = y * jax.nn.sigmoid(_dot(y.astype(BF16), wglu_ref[...]) + bglu_ref[...])
    y = y * _silu(proj_ref[slot, :, COL_ZB:COL_ZB + S5_WIDTH].astype(F32))
    return _dot(y.astype(BF16), wbout_ref[...])


def _block_kernel(nt,
                  xcur_ref, xnxt_ref, gpre_ref, w_ref, wgt_ref,
                  convw_ref, convb_ref, wq_ref, wk_ref, wv_ref, gbias_ref, headg_ref, skip_ref,
                  waout_ref,
                  perm_ref, permt_ref, bre_ref, bim_ref, are_ref, aim_ref, apre_ref, apim_ref,
                  cre_ref, cim_ref, dskip_ref, wglu_ref, bglu_ref, wbout_ref,
                  wo_ref, gpost_ref,
                  out_ref,
                  xn_ref, proj_ref, gates_ref,
                  ext_ref, c_ref, cb_ref, q_ref, k_ref, v_ref, hfin_ref, ct_ref, n_ref, m_ref,
                  sre_ref, sim_ref, xre_ref, xim_ref, yperm_ref):
    step = pl.program_id(0)
    slot = step % 2
    nslot = 1 - slot

    @pl.when(step == 0)
    def _():
        first = _Proj(xcur_ref, gpre_ref, w_ref, wgt_ref, xn_ref, proj_ref, gates_ref, 0)
        for j in range(MAIN_COLS // PROJ_BLOCK):
            first.block(j)
        first.gates()
        ext_ref[TILE:TILE + 8, :] = jnp.zeros((8, D_MODEL), F32)

    @pl.when(step % nt == 0)
    def _():
        ct_ref[...] = jnp.zeros_like(ct_ref)
        n_ref[...] = jnp.zeros_like(n_ref)
        m_ref[...] = jnp.full(m_ref.shape, -jnp.inf, F32)
        xre_ref[...] = jnp.zeros_like(xre_ref)
        xim_ref[...] = jnp.zeros_like(xim_ref)

    nxt = _Proj(xnxt_ref, gpre_ref, w_ref, wgt_ref, xn_ref, proj_ref, gates_ref, nslot)

    scan_args = (are_ref, aim_ref, apre_ref, apim_ref, sre_ref, sim_ref, xre_ref, xim_ref)
    chunk_args = (proj_ref, gates_ref, slot, gbias_ref, headg_ref, skip_ref,
                  c_ref, q_ref, k_ref, v_ref, hfin_ref, ct_ref, n_ref, m_ref)
    _s5_inputs(proj_ref, slot, perm_ref, bre_ref, bim_ref, sre_ref, sim_ref)
    _mlstm_conv(proj_ref, slot, step % nt == 0, convw_ref, convb_ref, ext_ref, c_ref, cb_ref)
    nxt.block(0)
    nxt.block(1)
    _mlstm_qkv(proj_ref, slot, wq_ref, wk_ref, wv_ref, cb_ref, q_ref, k_ref, v_ref)
    _s5_scan(0, *scan_args)
    _s5_scan(1, *scan_args)
    nxt.block(2)
    _mlstm_chunk(0, *chunk_args)
    nxt.block(3)
    _s5_scan(2, *scan_args)
    nxt.block(4)
    _s5_scan(3, *scan_args)
    nxt.block(5)
    _mlstm_chunk(1, *chunk_args)
    nxt.gates()
    y_a = _dot(hfin_ref[...], waout_ref[...])
    y_b = _s5_outputs(proj_ref, slot, permt_ref, cre_ref, cim_ref, dskip_ref, wglu_ref, bglu_ref,
                      wbout_ref, sre_ref, sim_ref, yperm_ref)

    ga = jax.nn.sigmoid(proj_ref[slot, :, COL_G:COL_G + D_MODEL].astype(F32))
    gb = jax.nn.sigmoid(proj_ref[slot, :, COL_G + D_MODEL:COL_G + 2 * D_MODEL].astype(F32))
    merged = (ga * y_a + gb * y_b).astype(BF16)
    out_ref[...] = xcur_ref[...] + _rms(_dot(merged, wo_ref[...])) * gpost_ref[...]


def _block(x2d, nt, weights):
    T = x2d.shape[0]
    nsteps = T // TILE

    def resident(a):
        nd = a.ndim
        return pl.BlockSpec(a.shape, lambda s: (0,) * nd, pipeline_mode=pl.Buffered(1))

    return pl.pallas_call(
        functools.partial(_block_kernel, nt),
        grid=(nsteps,),
        in_specs=[
            pl.BlockSpec((TILE, D_MODEL), lambda s: (s, 0)),
            pl.BlockSpec((TILE, D_MODEL), lambda s: (jnp.minimum(s + 1, nsteps - 1), 0)),
        ] + [resident(w) for w in weights],
        out_specs=pl.BlockSpec((TILE, D_MODEL), lambda s: (s, 0)),
        out_shape=jax.ShapeDtypeStruct((T, D_MODEL), F32),
        scratch_shapes=[
            pltpu.VMEM((TILE, D_MODEL), BF16),
            pltpu.VMEM((2, TILE, MAIN_COLS), BF16),
            pltpu.VMEM((2, GATE_ROWS, TILE), F32),
            pltpu.VMEM((TILE + 8, D_MODEL), F32),
            pltpu.VMEM((TILE, D_MODEL), F32),
            pltpu.VMEM((TILE, D_MODEL), BF16),
            pltpu.VMEM((TILE, D_MODEL), BF16),
            pltpu.VMEM((TILE, D_MODEL), BF16),
            pltpu.VMEM((TILE, D_MODEL), BF16),
            pltpu.VMEM((TILE, D_MODEL), BF16),
            pltpu.VMEM((HEADS, HEAD_DIM, HEAD_DIM), F32),
            pltpu.VMEM((HEADS, 1, HEAD_DIM), F32),
            pltpu.VMEM((HEADS, 1, 1), F32),
            pltpu.VMEM((TILE, S5_LANES), F32),
            pltpu.VMEM((TILE, S5_LANES), F32),
            pltpu.VMEM((1, S5_LANES), F32),
            pltpu.VMEM((1, S5_LANES), F32),
            pltpu.VMEM((TILE, S5_WIDTH), BF16),
        ],
        compiler_params=pltpu.CompilerParams(
            dimension_semantics=("arbitrary",), vmem_limit_bytes=VMEM_LIMIT),
        name="hybrid_block",
    )(x2d, x2d, *weights)


def _s5_params(lam_re, lam_im, log_dt, B_re, B_im, C_re, C_im):
    G, P, N = S5_GROUPS, S5_STATE, S5_GROUP
    dt = jnp.exp(log_dt.astype(F32))[:, None]
    lr = lam_re.astype(F32)
    li = lam_im.astype(F32)
    mag = jnp.exp(lr * dt)
    a_re = mag * jnp.cos(li * dt)
    a_im = mag * jnp.sin(li * dt)
    den = lr * lr + li * li
    nr = a_re - 1.0
    q_re = (nr * lr + a_im * li) / den
    q_im = (a_im * lr - nr * li) / den
    Br = B_re.astype(F32)
    Bi = B_im.astype(F32)
    bb_re = q_re[..., None] * Br - q_im[..., None] * Bi
    bb_im = q_re[..., None] * Bi + q_im[..., None] * Br

    gb = 128 // N
    eye_g = jnp.eye(gb, dtype=F32)

    def in_blocks(bb):
        return jnp.einsum('kgpn,gh->kgnhp', bb.reshape(G // gb, gb, P, N),
                          eye_g).reshape(G // gb, gb * N, gb * P).astype(BF16)

    def out_blocks(cc):
        return jnp.einsum('kgnp,gh->kgphn', cc.reshape(G // gb, gb, N, P),
                          eye_g).reshape(G // gb, gb * P, gb * N).astype(BF16)

    pr, pi = a_re, a_im
    for _ in range(int(math.log2(SUBLEN))):
        pr, pi = pr * pr - pi * pi, 2.0 * pr * pi
    flat = lambda a: a.reshape(1, G * P)
    return (in_blocks(bb_re), in_blocks(bb_im), flat(a_re), flat(a_im), flat(pr), flat(pi),
            out_blocks(C_re.astype(F32)), out_blocks(-C_im.astype(F32)))


def kernel(x, norm_pre_g, w_in, conv_w, conv_b, w_q, w_k, w_v, b_i, b_f, head_g, skip_a,
           w_a_out, lam_re, lam_im, log_dt, B_re, B_im, C_re, C_im, D_skip, w_glu, b_glu,
           w_b_out, w_o, norm_post_g):
    batch, seq, _ = x.shape
    depth = w_in.shape[0]
    T = batch * seq
    h = x.reshape(T, D_MODEL)

    p = np.arange(TILE)
    src = (p % SUBSEQ) * SUBLEN + p // SUBSEQ
    perm_np = np.zeros((TILE, TILE), np.float32)
    perm_np[p, src] = 1.0
    perm = jnp.asarray(perm_np, BF16)
    perm_t = jnp.asarray(perm_np.T, BF16)

    for l in range(depth):
        w = w_in[l]
        w_main = jnp.concatenate([w[:, 0:3072], w[:, 3080:3080 + 3072]], axis=1).astype(BF16)
        w_gate_t = jnp.zeros((GATE_ROWS, D_MODEL), F32).at[0:8].set(w[:, 3072:3080].T).astype(BF16)
        gbias = jnp.zeros((GATE_ROWS, 1), F32).at[0:4, 0].set(b_i[l]).at[4:8, 0].set(b_f[l])
        bin_re, bin_im, a_re, a_im, ap_re, ap_im, cout_re, cout_im = _s5_params(
            lam_re[l], lam_im[l], log_dt[l], B_re[l], B_im[l], C_re[l], C_im[l])
        weights = (
            norm_pre_g[l][None, :], _to_bf16(w_main), _to_bf16(w_gate_t),
            conv_w[l], conv_b[l][None, :], _to_bf16(w_q[l]),
            _to_bf16(w_k[l] * (HEAD_DIM ** -0.5)), _to_bf16(w_v[l]), gbias,
            head_g[l][None, :], skip_a[l][None, :], _to_bf16(w_a_out[l]),
            _to_bf16(perm), _to_bf16(perm_t), _to_bf16(bin_re), _to_bf16(bin_im), a_re, a_im, ap_re, ap_im,
            _to_bf16(cout_re), _to_bf16(cout_im),
            D_skip[l].reshape(1, S5_WIDTH), _to_bf16(w_glu[l]), b_glu[l][None, :],
            _to_bf16(w_b_out[l]),
            _to_bf16(w_o[l]), norm_post_g[l][None, :],
        )
        h = _block(h, seq // TILE, weights)
    return h.reshape(batch, seq, D_MODEL)
```

```python
import functools
import math

import jax
import jax.numpy as jnp
import numpy as np
from jax import lax
from jax.experimental import pallas as pl
from jax.experimental.pallas import tpu as pltpu

F32 = jnp.float32
BF16 = jnp.bfloat16

D_MODEL = 1024
HEADS = 4
HEAD_DIM = 256
CHUNK = 256
CONV_WIDTH = 4
S5_WIDTH = 512
S5_GROUP = 16
S5_GROUPS = 32
S5_STATE = 64
S5_LANES = S5_GROUPS * S5_STATE
NORM_EPS = 1e-6
GATE_ROWS = 16

COL_UA, COL_ZA, COL_OA, COL_UB, COL_ZB, COL_G = 0, 1024, 2048, 3072, 3584, 4096
MAIN_COLS = 6144
HALF_COLS = 3072
PROJ_BLOCK = 1024

TILE = 256
SUBSEQ = 8
SUBLEN = TILE // SUBSEQ
S5_LANE_BLOCK = 512
VMEM_LIMIT = 60 * 1024 * 1024


def _dot(a, b):
    return jnp.dot(a, b, preferred_element_type=F32)


def _dot_nt(a, b):
    return lax.dot_general(a, b, (((1,), (1,)), ((), ())), preferred_element_type=F32)


def _dot_tn(a, b):
    return lax.dot_general(a, b, (((0,), (0,)), ((), ())), preferred_element_type=F32)


def _to_bf16(w):
    return w.astype(BF16)


def _silu(x):
    return x * jax.nn.sigmoid(x)


def _rms(x):
    return x * lax.rsqrt(jnp.mean(x * x, axis=-1, keepdims=True) + NORM_EPS)


def _lane_cumsum(x):
    lane = lax.broadcasted_iota(jnp.int32, x.shape, 1)
    d = 1
    while d < x.shape[1]:
        x = x + jnp.where(lane >= d, pltpu.roll(x, d, axis=1), 0.0)
        d *= 2
    return x


class _Proj:
    def __init__(self, x_ref, g_ref, w_refs, wgt_ref, xn_ref, proj_ref, gates_ref, slot):
        xn_ref[...] = (_rms(x_ref[...]) * g_ref[...]).astype(BF16)
        self.w_refs, self.wgt_ref, self.xn_ref = w_refs, wgt_ref, xn_ref
        self.proj_ref, self.gates_ref, self.slot = proj_ref, gates_ref, slot

    def block(self, j):
        w_ref = self.w_refs[j * PROJ_BLOCK // HALF_COLS]
        wcols = slice(j * PROJ_BLOCK % HALF_COLS, j * PROJ_BLOCK % HALF_COLS + PROJ_BLOCK)
        cols = slice(j * PROJ_BLOCK, (j + 1) * PROJ_BLOCK)
        self.proj_ref[self.slot, :, cols] = _dot(self.xn_ref[...], w_ref[:, wcols]).astype(BF16)

    def gates(self):
        self.gates_ref[self.slot] = _dot_nt(self.wgt_ref[...], self.xn_ref[...])


def _mlstm_conv(proj_ref, slot, seq_start, convw_ref, convb_ref, ext_ref, c_ref, cb_ref):
    ext_ref[0:8, :] = jnp.where(seq_start, 0.0, ext_ref[TILE:TILE + 8, :])
    ext_ref[8:8 + TILE, :] = proj_ref[slot, :, COL_UA:COL_UA + D_MODEL].astype(F32)
    for h in range(HEADS):
        cols = slice(h * HEAD_DIM, (h + 1) * HEAD_DIM)
        acc = convb_ref[:, cols]
        for j in range(CONV_WIDTH):
            shift = CONV_WIDTH - 1 - j
            acc = acc + convw_ref[j:j + 1, cols] * ext_ref[8 - shift:8 - shift + TILE, cols]
        c = _silu(acc)
        c_ref[:, cols] = c
        cb_ref[:, cols] = c.astype(BF16)


def _mlstm_qkv(proj_ref, slot, wq_ref, wk_ref, wv_ref, cb_ref, q_ref, k_ref, v_ref):
    for h in range(HEADS):
        cols = slice(h * HEAD_DIM, (h + 1) * HEAD_DIM)
        cb = cb_ref[:, cols]
        q_ref[:, cols] = _dot(cb, wq_ref[h]).astype(BF16)
        k_ref[:, cols] = _dot(cb, wk_ref[h]).astype(BF16)
        ua = proj_ref[slot, :, COL_UA + h * HEAD_DIM:COL_UA + (h + 1) * HEAD_DIM]
        v_ref[:, cols] = _dot(ua, wv_ref[h]).astype(BF16)


def _mlstm_chunk(c_idx, proj_ref, gates_ref, slot, gbias_ref, headg_ref, skip_ref,
                 c_ref, q_ref, k_ref, v_ref, hfin_ref, ct_ref, n_ref, m_ref):
    row = lax.broadcasted_iota(jnp.int32, (CHUNK, CHUNK), 0)
    col = lax.broadcasted_iota(jnp.int32, (CHUNK, CHUNK), 1)
    eye = row == col
    causal = row >= col
    grow = lax.broadcasted_iota(jnp.int32, (GATE_ROWS, CHUNK), 0)
    glane = lax.broadcasted_iota(jnp.int32, (1, CHUNK), 1)

    rows = slice(c_idx * CHUNK, (c_idx + 1) * CHUNK)
    pre = gates_ref[slot, :, rows] + gbias_ref[...]
    logsig = jnp.minimum(pre, 0.0) - jnp.log1p(jnp.exp(-jnp.abs(pre)))
    gate = jnp.where(grow >= HEADS, logsig, pre)
    bsum = _lane_cumsum(gate)
    for h in range(HEADS):
        cols = slice(h * HEAD_DIM, (h + 1) * HEAD_DIM)
        ig_r = gate[h:h + 1, :]
        b_r = bsum[HEADS + h:HEADS + h + 1, :]
        b_c = jnp.sum(jnp.where(eye, b_r, 0.0), axis=-1, keepdims=True)
        ig_c = jnp.sum(jnp.where(eye, ig_r, 0.0), axis=-1, keepdims=True)
        b_last = jnp.sum(jnp.where(glane == CHUNK - 1, b_r, 0.0), axis=-1, keepdims=True)
        m_prev = m_ref[h]
        q = q_ref[rows, cols]
        k = k_ref[rows, cols]
        v = v_ref[rows, cols]

        log_d = jnp.where(causal, b_c - b_r + ig_r, -jnp.inf)
        log_inter = b_c + m_prev
        m_t = jnp.maximum(log_inter, jnp.max(log_d, axis=-1, keepdims=True))
        dmat = jnp.exp(log_d - m_t)
        inter = jnp.exp(log_inter - m_t)
        s = _dot_nt(q, k) * dmat
        ct_prev = ct_ref[h]
        num = _dot(s.astype(BF16), v) + inter * _dot(q, ct_prev.astype(BF16))
        qn = jnp.sum(q.astype(F32) * n_ref[h], axis=-1, keepdims=True)
        den = jnp.sum(s, axis=-1, keepdims=True) + inter * qn
        hh = num * (1.0 / jnp.maximum(jnp.abs(den), jnp.exp(-m_t)))

        w_end_r = b_last - b_r + ig_r
        m_loc = jnp.max(w_end_r, axis=-1, keepdims=True)
        m_new = jnp.maximum(b_last + m_prev, m_loc)
        decay = jnp.exp(b_last + m_prev - m_new)
        e_c = jnp.exp(b_last - b_c + ig_c - m_new)
        kw = k.astype(F32) * e_c
        ct_ref[h] = decay * ct_prev + _dot_tn(kw.astype(BF16), v)
        n_ref[h] = decay * n_ref[h] + jnp.sum(kw, axis=0, keepdims=True)
        m_ref[h] = m_new

        o_pre = proj_ref[slot, rows, COL_OA + h * HEAD_DIM:COL_OA + (h + 1) * HEAD_DIM]
        z_pre = proj_ref[slot, rows, COL_ZA + h * HEAD_DIM:COL_ZA + (h + 1) * HEAD_DIM]
        hh = _rms(jax.nn.sigmoid(o_pre.astype(F32)) * hh)
        hh = hh * headg_ref[:, cols] + skip_ref[:, cols] * c_ref[rows, cols]
        hh = hh * _silu(z_pre.astype(F32))
        hfin_ref[rows, cols] = hh.astype(BF16)


def _s5_inputs(proj_ref, slot, perm_ref, bre_ref, bim_ref, sre_ref, sim_ref):
    u_perm = _dot(perm_ref[...], proj_ref[slot, :, COL_UB:COL_UB + S5_WIDTH]).astype(BF16)
    for kb in range(S5_WIDTH // 128):
        ukb = u_perm[:, kb * 128:(kb + 1) * 128]
        sre_ref[:, kb * 512:(kb + 1) * 512] = _dot(ukb, bre_ref[kb])
        sim_ref[:, kb * 512:(kb + 1) * 512] = _dot(ukb, bim_ref[kb])


def _s5_scan(lb, are_ref, aim_ref, apre_ref, apim_ref, sre_ref, sim_ref, xre_ref, xim_ref):
    lanes = slice(lb * S5_LANE_BLOCK, (lb + 1) * S5_LANE_BLOCK)
    shape = (SUBSEQ, S5_LANE_BLOCK)
    ar = jnp.broadcast_to(are_ref[:, lanes], shape)
    ai = jnp.broadcast_to(aim_ref[:, lanes], shape)

    def step(j, sr, si):
        r8 = slice(j * SUBSEQ, (j + 1) * SUBSEQ)
        br = sre_ref[r8, lanes]
        bi = sim_ref[r8, lanes]
        return ar * sr - ai * si + br, ar * si + ai * sr + bi

    er = jnp.zeros(shape, F32)
    ei = jnp.zeros(shape, F32)
    for j in range(SUBLEN):
        er, ei = step(j, er, ei)

    sub = lax.broadcasted_iota(jnp.int32, shape, 0)
    pr = apre_ref[:, lanes]
    pi = apim_ref[:, lanes]
    xr = xre_ref[:, lanes]
    xi = xim_ref[:, lanes]
    sr = jnp.zeros(shape, F32)
    si = jnp.zeros(shape, F32)
    for r in range(SUBSEQ):
        sr = jnp.where(sub == r, xr, sr)
        si = jnp.where(sub == r, xi, si)
        xr, xi = (pr * xr - pi * xi + er[r:r + 1, :], pr * xi + pi * xr + ei[r:r + 1, :])
    xre_ref[:, lanes] = xr
    xim_ref[:, lanes] = xi

    for j in range(SUBLEN):
        sr, si = step(j, sr, si)
        r8 = slice(j * SUBSEQ, (j + 1) * SUBSEQ)
        sre_ref[r8, lanes] = sr
        sim_ref[r8, lanes] = si


def _s5_outputs(proj_ref, slot, permt_ref, cre_ref, cim_ref, dskip_ref, wglu_ref, bglu_ref,
                wbout_ref, sre_ref, sim_ref, yperm_ref):
    for ob in range(S5_WIDTH // 128):
        lanes = slice(ob * 512, (ob + 1) * 512)
        yperm_ref[:, ob * 128:(ob + 1) * 128] = (
            _dot(sre_ref[:, lanes].astype(BF16), cre_ref[ob])
            + _dot(sim_ref[:, lanes].astype(BF16), cim_ref[ob])).astype(BF16)
    u = proj_ref[slot, :, COL_UB:COL_UB + S5_WIDTH].astype(F32)
    y = _dot(permt_ref[...], yperm_ref[...]) + dskip_ref[...] * u
    y = 0.5 * y * (1.0 + jnp.tanh(math.sqrt(2.0 / math.pi) * (y + 0.044715 * (y * y * y))))
    y = y * jax.nn.sigmoid(_dot(y.astype(BF16), wglu_ref[...]) + bglu_ref[...])
    y = y * _silu(proj_ref[slot, :, COL_ZB:COL_ZB + S5_WIDTH].astype(F32))
    return _dot(y.astype(BF16), wbout_ref[...])


def _block_kernel(nt,
                  xcur_ref, xnxt_ref, gpre_ref, wa_ref, wb_ref, wgt_ref,
                  convw_ref, convb_ref, wq_ref, wk_ref, wv_ref, gbias_ref, headg_ref, skip_ref,
                  waout_ref,
                  perm_ref, permt_ref, bre_ref, bim_ref, are_ref, aim_ref, apre_ref, apim_ref,
                  cre_ref, cim_ref, dskip_ref, wglu_ref, bglu_ref, wbout_ref,
                  wo_ref, gpost_ref,
                  out_ref,
                  xn_ref, proj_ref, gates_ref,
                  ext_ref, c_ref, cb_ref, q_ref, k_ref, v_ref, hfin_ref, ct_ref, n_ref, m_ref,
                  sre_ref, sim_ref, xre_ref, xim_ref, yperm_ref):
    step = pl.program_id(0)
    slot = step % 2
    nslot = 1 - slot

    @pl.when(step == 0)
    def _():
        first = _Proj(xcur_ref, gpre_ref, (wa_ref, wb_ref), wgt_ref, xn_ref, proj_ref, gates_ref, 0)
        for j in range(MAIN_COLS // PROJ_BLOCK):
            first.block(j)
        first.gates()
        ext_ref[TILE:TILE + 8, :] = jnp.zeros((8, D_MODEL), F32)

    @pl.when(step % nt == 0)
    def _():
        ct_ref[...] = jnp.zeros_like(ct_ref)
        n_ref[...] = jnp.zeros_like(n_ref)
        m_ref[...] = jnp.full(m_ref.shape, -jnp.inf, F32)
        xre_ref[...] = jnp.zeros_like(xre_ref)
        xim_ref[...] = jnp.zeros_like(xim_ref)

    nxt = _Proj(xnxt_ref, gpre_ref, (wa_ref, wb_ref), wgt_ref, xn_ref, proj_ref, gates_ref, nslot)

    scan_args = (are_ref, aim_ref, apre_ref, apim_ref, sre_ref, sim_ref, xre_ref, xim_ref)
    chunk_args = (proj_ref, gates_ref, slot, gbias_ref, headg_ref, skip_ref,
                  c_ref, q_ref, k_ref, v_ref, hfin_ref, ct_ref, n_ref, m_ref)
    _s5_inputs(proj_ref, slot, perm_ref, bre_ref, bim_ref, sre_ref, sim_ref)
    _mlstm_conv(proj_ref, slot, step % nt == 0, convw_ref, convb_ref, ext_ref, c_ref, cb_ref)
    nxt.block(0)
    nxt.block(1)
    _mlstm_qkv(proj_ref, slot, wq_ref, wk_ref, wv_ref, cb_ref, q_ref, k_ref, v_ref)
    _s5_scan(0, *scan_args)
    _s5_scan(1, *scan_args)
    nxt.block(2)
    _mlstm_chunk(0, *chunk_args)
    nxt.block(3)
    _s5_scan(2, *scan_args)
    nxt.block(4)
    _s5_scan(3, *scan_args)
    nxt.block(5)
    nxt.gates()
    y_a = _dot(hfin_ref[...], waout_ref[...])
    y_b = _s5_outputs(proj_ref, slot, permt_ref, cre_ref, cim_ref, dskip_ref, wglu_ref, bglu_ref,
                      wbout_ref, sre_ref, sim_ref, yperm_ref)

    ga = jax.nn.sigmoid(proj_ref[slot, :, COL_G:COL_G + D_MODEL].astype(F32))
    gb = jax.nn.sigmoid(proj_ref[slot, :, COL_G + D_MODEL:COL_G + 2 * D_MODEL].astype(F32))
    merged = (ga * y_a + gb * y_b).astype(BF16)
    out_ref[...] = xcur_ref[...] + _rms(_dot(merged, wo_ref[...])) * gpost_ref[...]


def _block(x2d, nt, weights):
    T = x2d.shape[0]
    nsteps = T // TILE

    def resident(a):
        nd = a.ndim
        return pl.BlockSpec(a.shape, lambda s: (0,) * nd, pipeline_mode=pl.Buffered(1))

    return pl.pallas_call(
        functools.partial(_block_kernel, nt),
        grid=(nsteps,),
        in_specs=[
            pl.BlockSpec((TILE, D_MODEL), lambda s: (s, 0)),
            pl.BlockSpec((TILE, D_MODEL), lambda s: (jnp.minimum(s + 1, nsteps - 1), 0)),
        ] + [resident(w) for w in weights],
        out_specs=pl.BlockSpec((TILE, D_MODEL), lambda s: (s, 0)),
        out_shape=jax.ShapeDtypeStruct((T, D_MODEL), F32),
        scratch_shapes=[
            pltpu.VMEM((TILE, D_MODEL), BF16),
            pltpu.VMEM((2, TILE, MAIN_COLS), BF16),
            pltpu.VMEM((2, GATE_ROWS, TILE), F32),
            pltpu.VMEM((TILE + 8, D_MODEL), F32),
            pltpu.VMEM((TILE, D_MODEL), F32),
            pltpu.VMEM((TILE, D_MODEL), BF16),
            pltpu.VMEM((TILE, D_MODEL), BF16),
            pltpu.VMEM((TILE, D_MODEL), BF16),
            pltpu.VMEM((TILE, D_MODEL), BF16),
            pltpu.VMEM((TILE, D_MODEL), BF16),
            pltpu.VMEM((HEADS, HEAD_DIM, HEAD_DIM), F32),
            pltpu.VMEM((HEADS, 1, HEAD_DIM), F32),
            pltpu.VMEM((HEADS, 1, 1), F32),
            pltpu.VMEM((TILE, S5_LANES), F32),
            pltpu.VMEM((TILE, S5_LANES), F32),
            pltpu.VMEM((1, S5_LANES), F32),
            pltpu.VMEM((1, S5_LANES), F32),
            pltpu.VMEM((TILE, S5_WIDTH), BF16),
        ],
        compiler_params=pltpu.CompilerParams(
            dimension_semantics=("arbitrary",), vmem_limit_bytes=VMEM_LIMIT),
        name="hybrid_block",
    )(x2d, x2d, *weights)


def _s5_params(lam_re, lam_im, log_dt, B_re, B_im, C_re, C_im):
    G, P, N = S5_GROUPS, S5_STATE, S5_GROUP
    dt = jnp.exp(log_dt.astype(F32))[:, None]
    lr = lam_re.astype(F32)
    li = lam_im.astype(F32)
    mag = jnp.exp(lr * dt)
    a_re = mag * jnp.cos(li * dt)
    a_im = mag * jnp.sin(li * dt)
    den = lr * lr + li * li
    nr = a_re - 1.0
    q_re = (nr * lr + a_im * li) / den
    q_im = (a_im * lr - nr * li) / den
    Br = B_re.astype(F32)
    Bi = B_im.astype(F32)
    bb_re = q_re[..., None] * Br - q_im[..., None] * Bi
    bb_im = q_re[..., None] * Bi + q_im[..., None] * Br

    gb = 128 // N
    eye_g = jnp.eye(gb, dtype=F32)

    def in_blocks(bb):
        return jnp.einsum('kgpn,gh->kgnhp', bb.reshape(G // gb, gb, P, N),
                          eye_g).reshape(G // gb, gb * N, gb * P).astype(BF16)

    def out_blocks(cc):
        return jnp.einsum('kgnp,gh->kgphn', cc.reshape(G // gb, gb, N, P),
                          eye_g).reshape(G // gb, gb * P, gb * N).astype(BF16)

    pr, pi = a_re, a_im
    for _ in range(int(math.log2(SUBLEN))):
        pr, pi = pr * pr - pi * pi, 2.0 * pr * pi
    flat = lambda a: a.reshape(1, G * P)
    return (in_blocks(bb_re), in_blocks(bb_im), flat(a_re), flat(a_im), flat(pr), flat(pi),
            out_blocks(C_re.astype(F32)), out_blocks(-C_im.astype(F32)))


def kernel(x, norm_pre_g, w_in, conv_w, conv_b, w_q, w_k, w_v, b_i, b_f, head_g, skip_a,
           w_a_out, lam_re, lam_im, log_dt, B_re, B_im, C_re, C_im, D_skip, w_glu, b_glu,
           w_b_out, w_o, norm_post_g):
    batch, seq, _ = x.shape
    depth = w_in.shape[0]
    T = batch * seq
    h = x.reshape(T, D_MODEL)

    p = np.arange(TILE)
    src = (p % SUBSEQ) * SUBLEN + p // SUBSEQ
    perm_np = np.zeros((TILE, TILE), np.float32)
    perm_np[p, src] = 1.0
    perm = jnp.asarray(perm_np, BF16)
    perm_t = jnp.asarray(perm_np.T, BF16)

    for l in range(depth):
        w = w_in[l]
        w_a = w[:, 0:HALF_COLS].astype(BF16)
        w_b = w[:, HALF_COLS + 8:2 * HALF_COLS + 8].astype(BF16)
        w_gate_t = jnp.pad(w[:, HALF_COLS:HALF_COLS + 128][:, 0:8].T.astype(BF16),
                           ((0, GATE_ROWS - 8), (0, 0)))
        gbias = jnp.zeros((GATE_ROWS, 1), F32).at[0:4, 0].set(b_i[l]).at[4:8, 0].set(b_f[l])
        bin_re, bin_im, a_re, a_im, ap_re, ap_im, cout_re, cout_im = _s5_params(
            lam_re[l], lam_im[l], log_dt[l], B_re[l], B_im[l], C_re[l], C_im[l])
        weights = (
            norm_pre_g[l][None, :], w_a, w_b, w_gate_t,
            conv_w[l], conv_b[l][None, :], _to_bf16(w_q[l]),
            _to_bf16(w_k[l] * (HEAD_DIM ** -0.5)), _to_bf16(w_v[l]), gbias,
            head_g[l][None, :], skip_a[l][None, :], _to_bf16(w_a_out[l]),
            _to_bf16(perm), _to_bf16(perm_t), _to_bf16(bin_re), _to_bf16(bin_im), a_re, a_im, ap_re, ap_im,
            _to_bf16(cout_re), _to_bf16(cout_im),
            D_skip[l].reshape(1, S5_WIDTH), _to_bf16(w_glu[l]), b_glu[l][None, :],
            _to_bf16(w_b_out[l]),
            _to_bf16(w_o[l]), norm_post_g[l][None, :],
        )
        h = _block(h, seq // TILE, weights)
    return h.reshape(batch, seq, D_MODEL)
```

```python
import functools
import math

import jax
import jax.numpy as jnp
import numpy as np
from jax import lax
from jax.experimental import pallas as pl
from jax.experimental.pallas import tpu as pltpu

F32 = jnp.float32
BF16 = jnp.bfloat16

D_MODEL = 1024
HEADS = 4
HEAD_DIM = 256
CHUNK = 256
CONV_WIDTH = 4
S5_WIDTH = 512
S5_GROUP = 16
S5_GROUPS = 32
S5_STATE = 64
S5_LANES = S5_GROUPS * S5_STATE
NORM_EPS = 1e-6
GATE_ROWS = 16

COL_UA, COL_ZA, COL_OA, COL_UB, COL_ZB, COL_G = 0, 1024, 2048, 3072, 3584, 4096
MAIN_COLS = 6144
HALF_COLS = 3072
PROJ_BLOCK = 1024

TILE = 256
SUBSEQ = 8
SUBLEN = TILE // SUBSEQ
S5_LANE_BLOCK = 1024
VMEM_LIMIT = 60 * 1024 * 1024


def _dot(a, b):
    return jnp.dot(a, b, preferred_element_type=F32)


def _dot_nt(a, b):
    return lax.dot_general(a, b, (((1,), (1,)), ((), ())), preferred_element_type=F32)


def _dot_tn(a, b):
    return lax.dot_general(a, b, (((0,), (0,)), ((), ())), preferred_element_type=F32)


def _to_bf16(w):
    return w.astype(BF16)


def _silu(x):
    return x * jax.nn.sigmoid(x)


def _rms(x):
    return x * lax.rsqrt(jnp.mean(x * x, axis=-1, keepdims=True) + NORM_EPS)


def _lane_cumsum(x):
    lane = lax.broadcasted_iota(jnp.int32, x.shape, 1)
    d = 1
    while d < x.shape[1]:
        x = x + jnp.where(lane >= d, pltpu.roll(x, d, axis=1), 0.0)
        d *= 2
    return x


class _Proj:
    def __init__(self, x_ref, g_ref, w_refs, wgt_ref, xn_ref, proj_ref, gates_ref, slot):
        xn_ref[...] = (_rms(x_ref[...]) * g_ref[...]).astype(BF16)
        self.w_refs, self.wgt_ref, self.xn_ref = w_refs, wgt_ref, xn_ref
        self.proj_ref, self.gates_ref, self.slot = proj_ref, gates_ref, slot

    def block(self, j):
        w_ref = self.w_refs[j * PROJ_BLOCK // HALF_COLS]
        wcols = slice(j * PROJ_BLOCK % HALF_COLS, j * PROJ_BLOCK % HALF_COLS + PROJ_BLOCK)
        cols = slice(j * PROJ_BLOCK, (j + 1) * PROJ_BLOCK)
        self.proj_ref[self.slot, :, cols] = _dot(self.xn_ref[...], w_ref[:, wcols]).astype(BF16)

    def gates(self):
        self.gates_ref[self.slot] = _dot_nt(self.wgt_ref[...], self.xn_ref[...])


def _mlstm_conv(proj_ref, slot, seq_start, convw_ref, convb_ref, ext_ref, c_ref, cb_ref):
    ext_ref[0:8, :] = jnp.where(seq_start, 0.0, ext_ref[TILE:TILE + 8, :])
    ext_ref[8:8 + TILE, :] = proj_ref[slot, :, COL_UA:COL_UA + D_MODEL].astype(F32)
    for h in range(HEADS):
        cols = slice(h * HEAD_DIM, (h + 1) * HEAD_DIM)
        acc = convb_ref[:, cols]
        for j in range(CONV_WIDTH):
            shift = CONV_WIDTH - 1 - j
            acc = acc + convw_ref[j:j + 1, cols] * ext_ref[8 - shift:8 - shift + TILE, cols]
        c = _silu(acc)
        c_ref[:, cols] = c
        cb_ref[:, cols] = c.astype(BF16)


def _mlstm_qkv(proj_ref, slot, wq_ref, wk_ref, wv_ref, cb_ref, q_ref, k_ref, v_ref):
    for h in range(HEADS):
        cols = slice(h * HEAD_DIM, (h + 1) * HEAD_DIM)
        cb = cb_ref[:, cols]
        q_ref[:, cols] = _dot(cb, wq_ref[h]).astype(BF16)
        k_ref[:, cols] = _dot(cb, wk_ref[h]).astype(BF16)
        ua = proj_ref[slot, :, COL_UA + h * HEAD_DIM:COL_UA + (h + 1) * HEAD_DIM]
        v_ref[:, cols] = _dot(ua, wv_ref[h]).astype(BF16)


def _mlstm_chunk(c_idx, proj_ref, gates_ref, slot, gbias_ref, headg_ref, skip_ref,
                 c_ref, q_ref, k_ref, v_ref, hfin_ref, ct_ref, n_ref, m_ref):
    row = lax.broadcasted_iota(jnp.int32, (CHUNK, CHUNK), 0)
    col = lax.broadcasted_iota(jnp.int32, (CHUNK, CHUNK), 1)
    eye = row == col
    causal = row >= col
    grow = lax.broadcasted_iota(jnp.int32, (GATE_ROWS, CHUNK), 0)
    glane = lax.broadcasted_iota(jnp.int32, (1, CHUNK), 1)

    rows = slice(c_idx * CHUNK, (c_idx + 1) * CHUNK)
    pre = gates_ref[slot, :, rows] + gbias_ref[...]
    logsig = jnp.minimum(pre, 0.0) - jnp.log1p(jnp.exp(-jnp.abs(pre)))
    gate = jnp.where(grow >= HEADS, logsig, pre)
    bsum = _lane_cumsum(gate)
    for h in range(HEADS):
        cols = slice(h * HEAD_DIM, (h + 1) * HEAD_DIM)
        ig_r = gate[h:h + 1, :]
        b_r = bsum[HEADS + h:HEADS + h + 1, :]
        b_c = jnp.sum(jnp.where(eye, b_r, 0.0), axis=-1, keepdims=True)
        ig_c = jnp.sum(jnp.where(eye, ig_r, 0.0), axis=-1, keepdims=True)
        b_last = jnp.sum(jnp.where(glane == CHUNK - 1, b_r, 0.0), axis=-1, keepdims=True)
        m_prev = m_ref[h]
        q = q_ref[rows, cols]
        k = k_ref[rows, cols]
        v = v_ref[rows, cols]

        log_d = jnp.where(causal, b_c - b_r + ig_r, -jnp.inf)
        log_inter = b_c + m_prev
        m_t = jnp.maximum(log_inter, jnp.max(log_d, axis=-1, keepdims=True))
        dmat = jnp.exp(log_d - m_t)
        inter = jnp.exp(log_inter - m_t)
        s = _dot_nt(q, k) * dmat
        ct_prev = ct_ref[h]
        num = _dot(s.astype(BF16), v) + inter * _dot(q, ct_prev.astype(BF16))
        qn = jnp.sum(q.astype(F32) * n_ref[h], axis=-1, keepdims=True)
        den = jnp.sum(s, axis=-1, keepdims=True) + inter * qn
        hh = num * (1.0 / jnp.maximum(jnp.abs(den), jnp.exp(-m_t)))

        w_end_r = b_last - b_r + ig_r
        m_loc = jnp.max(w_end_r, axis=-1, keepdims=True)
        m_new = jnp.maximum(b_last + m_prev, m_loc)
        decay = jnp.exp(b_last + m_prev - m_new)
        e_c = jnp.exp(b_last - b_c + ig_c - m_new)
        kw = k.astype(F32) * e_c
        ct_ref[h] = decay * ct_prev + _dot_tn(kw.astype(BF16), v)
        n_ref[h] = decay * n_ref[h] + jnp.sum(kw, axis=0, keepdims=True)
        m_ref[h] = m_new

        o_pre = proj_ref[slot, rows, COL_OA + h * HEAD_DIM:COL_OA + (h + 1) * HEAD_DIM]
        z_pre = proj_ref[slot, rows, COL_ZA + h * HEAD_DIM:COL_ZA + (h + 1) * HEAD_DIM]
        hh = _rms(jax.nn.sigmoid(o_pre.astype(F32)) * hh)
        hh = hh * headg_ref[:, cols] + skip_ref[:, cols] * c_ref[rows, cols]
        hh = hh * _silu(z_pre.astype(F32))
        hfin_ref[rows, cols] = hh.astype(BF16)


def _s5_inputs(proj_ref, slot, perm_ref, bre_ref, bim_ref, sre_ref, sim_ref):
    u_perm = _dot(perm_ref[...], proj_ref[slot, :, COL_UB:COL_UB + S5_WIDTH]).astype(BF16)
    for kb in range(S5_WIDTH // 128):
        ukb = u_perm[:, kb * 128:(kb + 1) * 128]
        sre_ref[:, kb * 512:(kb + 1) * 512] = _dot(ukb, bre_ref[kb])
        sim_ref[:, kb * 512:(kb + 1) * 512] = _dot(ukb, bim_ref[kb])


def _s5_scan(lb, are_ref, aim_ref, apre_ref, apim_ref, sre_ref, sim_ref, xre_ref, xim_ref):
    lanes = slice(lb * S5_LANE_BLOCK, (lb + 1) * S5_LANE_BLOCK)
    shape = (SUBSEQ, S5_LANE_BLOCK)
    ar = jnp.broadcast_to(are_ref[:, lanes], shape)
    ai = jnp.broadcast_to(aim_ref[:, lanes], shape)

    def step(j, sr, si):
        r8 = slice(j * SUBSEQ, (j + 1) * SUBSEQ)
        br = sre_ref[r8, lanes]
        bi = sim_ref[r8, lanes]
        return ar * sr - ai * si + br, ar * si + ai * sr + bi

    er = jnp.zeros(shape, F32)
    ei = jnp.zeros(shape, F32)
    for j in range(SUBLEN):
        er, ei = step(j, er, ei)

    sub = lax.broadcasted_iota(jnp.int32, shape, 0)
    pr = apre_ref[:, lanes]
    pi = apim_ref[:, lanes]
    xr = xre_ref[:, lanes]
    xi = xim_ref[:, lanes]
    sr = jnp.zeros(shape, F32)
    si = jnp.zeros(shape, F32)
    for r in range(SUBSEQ):
        sr = jnp.where(sub == r, xr, sr)
        si = jnp.where(sub == r, xi, si)
        xr, xi = (pr * xr - pi * xi + er[r:r + 1, :], pr * xi + pi * xr + ei[r:r + 1, :])
    xre_ref[:, lanes] = xr
    xim_ref[:, lanes] = xi

    for j in range(SUBLEN):
        sr, si = step(j, sr, si)
        r8 = slice(j * SUBSEQ, (j + 1) * SUBSEQ)
        sre_ref[r8, lanes] = sr
        sim_ref[r8, lanes] = si


def _s5_outputs(proj_ref, slot, permt_ref, cre_ref, cim_ref, dskip_ref, wglu_ref, bglu_ref,
                wbout_ref, sre_ref, sim_ref, yperm_ref):
    for ob in range(S5_WIDTH // 128):
        lanes = slice(ob * 512, (ob + 1) * 512)
        yperm_ref[:, ob * 128:(ob + 1) * 128] = (
            _dot(sre_ref[:, lanes].astype(BF16), cre_ref[ob])
            + _dot(sim_ref[:, lanes].astype(BF16), cim_ref[ob])).astype(BF16)
    u = proj_ref[slot, :, COL_UB:COL_UB + S5_WIDTH].astype(F32)
    y = _dot(permt_ref[...], yperm_ref[...]) + dskip_ref[...] * u
    y = 0.5 * y * (1.0 + jnp.tanh(math.sqrt(2.0 / math.pi) * (y + 0.044715 * (y * y * y))))
    y = y * jax.nn.sigmoid(_dot(y.astype(BF16), wglu_ref[...]) + bglu_ref[...])
    y = y * _silu(proj_ref[slot, :, COL_ZB:COL_ZB + S5_WIDTH].astype(F32))
    return _dot(y.astype(BF16), wbout_ref[...])


def _block_kernel(nt,
                  xcur_ref, xnxt_ref, gpre_ref, wa_ref, wb_ref, wgt_ref,
                  convw_ref, convb_ref, wq_ref, wk_ref, wv_ref, gbias_ref, headg_ref, skip_ref,
                  waout_ref,
                  perm_ref, permt_ref, bre_ref, bim_ref, are_ref, aim_ref, apre_ref, apim_ref,
                  cre_ref, cim_ref, dskip_ref, wglu_ref, bglu_ref, wbout_ref,
                  wo_ref, gpost_ref,
                  out_ref,
                  xn_ref, proj_ref, gates_ref,
                  ext_ref, c_ref, cb_ref, q_ref, k_ref, v_ref, hfin_ref, ct_ref, n_ref, m_ref,
                  sre_ref, sim_ref, xre_ref, xim_ref, yperm_ref):
    step = pl.program_id(0)
    slot = step % 2
    nslot = 1 - slot

    @pl.when(step == 0)
    def _():
        first = _Proj(xcur_ref, gpre_ref, (wa_ref, wb_ref), wgt_ref, xn_ref, proj_ref, gates_ref, 0)
        for j in range(MAIN_COLS // PROJ_BLOCK):
            first.block(j)
        first.gates()
        ext_ref[TILE:TILE + 8, :] = jnp.zeros((8, D_MODEL), F32)
        _mlstm_conv(proj_ref, 0, True, convw_ref, convb_ref, ext_ref, c_ref, cb_ref)

    @pl.when(step % nt == 0)
    def _():
        ct_ref[...] = jnp.zeros_like(ct_ref)
        n_ref[...] = jnp.zeros_like(n_ref)
        m_ref[...] = jnp.full(m_ref.shape, -jnp.inf, F32)
        xre_ref[...] = jnp.zeros_like(xre_ref)
        xim_ref[...] = jnp.zeros_like(xim_ref)

    nxt = _Proj(xnxt_ref, gpre_ref, (wa_ref, wb_ref), wgt_ref, xn_ref, proj_ref, gates_ref, nslot)

    scan_args = (are_ref, aim_ref, apre_ref, apim_ref, sre_ref, sim_ref, xre_ref, xim_ref)
    chunk_args = (proj_ref, gates_ref, slot, gbias_ref, headg_ref, skip_ref,
                  c_ref, q_ref, k_ref, v_ref, hfin_ref, ct_ref, n_ref, m_ref)
    _mlstm_qkv(proj_ref, slot, wq_ref, wk_ref, wv_ref, cb_ref, q_ref, k_ref, v_ref)
    _s5_inputs(proj_ref, slot, perm_ref, bre_ref, bim_ref, sre_ref, sim_ref)
    nxt.block(0)
    nxt.block(1)
    _s5_scan(0, *scan_args)
    _mlstm_chunk(0, *chunk_args)
    nxt.block(2)
    nxt.block(3)
    _s5_scan(1, *scan_args)
    _mlstm_conv(proj_ref, nslot, (step + 1) % nt == 0, convw_ref, convb_ref, ext_ref, c_ref, cb_ref)
    nxt.block(4)
    nxt.block(5)
    nxt.gates()
    y_a = _dot(hfin_ref[...], waout_ref[...])
    y_b = _s5_outputs(proj_ref, slot, permt_ref, cre_ref, cim_ref, dskip_ref, wglu_ref, bglu_ref,
                      wbout_ref, sre_ref, sim_ref, yperm_ref)

    ga = jax.nn.sigmoid(proj_ref[slot, :, COL_G:COL_G + D_MODEL].astype(F32))
    gb = jax.nn.sigmoid(proj_ref[slot, :, COL_G + D_MODEL:COL_G + 2 * D_MODEL].astype(F32))
    merged = (ga * y_a + gb * y_b).astype(BF16)
    out_ref[...] = xcur_ref[...] + _rms(_dot(merged, wo_ref[...])) * gpost_ref[...]


def _block(x2d, nt, weights):
    T = x2d.shape[0]
    nsteps = T // TILE

    def resident(a):
        nd = a.ndim
        return pl.BlockSpec(a.shape, lambda s: (0,) * nd, pipeline_mode=pl.Buffered(1))

    return pl.pallas_call(
        functools.partial(_block_kernel, nt),
        grid=(nsteps,),
        in_specs=[
            pl.BlockSpec((TILE, D_MODEL), lambda s: (s, 0)),
            pl.BlockSpec((TILE, D_MODEL), lambda s: (jnp.minimum(s + 1, nsteps - 1), 0)),
        ] + [resident(w) for w in weights],
        out_specs=pl.BlockSpec((TILE, D_MODEL), lambda s: (s, 0)),
        out_shape=jax.ShapeDtypeStruct((T, D_MODEL), F32),
        scratch_shapes=[
            pltpu.VMEM((TILE, D_MODEL), BF16),
            pltpu.VMEM((2, TILE, MAIN_COLS), BF16),
            pltpu.VMEM((2, GATE_ROWS, TILE), F32),
            pltpu.VMEM((TILE + 8, D_MODEL), F32),
            pltpu.VMEM((TILE, D_MODEL), F32),
            pltpu.VMEM((TILE, D_MODEL), BF16),
            pltpu.VMEM((TILE, D_MODEL), BF16),
            pltpu.VMEM((TILE, D_MODEL), BF16),
            pltpu.VMEM((TILE, D_MODEL), BF16),
            pltpu.VMEM((TILE, D_MODEL), BF16),
            pltpu.VMEM((HEADS, HEAD_DIM, HEAD_DIM), F32),
            pltpu.VMEM((HEADS, 1, HEAD_DIM), F32),
            pltpu.VMEM((HEADS, 1, 1), F32),
            pltpu.VMEM((TILE, S5_LANES), F32),
            pltpu.VMEM((TILE, S5_LANES), F32),
            pltpu.VMEM((1, S5_LANES), F32),
            pltpu.VMEM((1, S5_LANES), F32),
            pltpu.VMEM((TILE, S5_WIDTH), BF16),
        ],
        compiler_params=pltpu.CompilerParams(
            dimension_semantics=("arbitrary",), vmem_limit_bytes=VMEM_LIMIT),
        name="hybrid_block",
    )(x2d, x2d, *weights)


def _s5_params(lam_re, lam_im, log_dt, B_re, B_im, C_re, C_im):
    G, P, N = S5_GROUPS, S5_STATE, S5_GROUP
    dt = jnp.exp(log_dt.astype(F32))[:, None]
    lr = lam_re.astype(F32)
    li = lam_im.astype(F32)
    mag = jnp.exp(lr * dt)
    a_re = mag * jnp.cos(li * dt)
    a_im = mag * jnp.sin(li * dt)
    den = lr * lr + li * li
    nr = a_re - 1.0
    q_re = (nr * lr + a_im * li) / den
    q_im = (a_im * lr - nr * li) / den
    Br = B_re.astype(F32)
    Bi = B_im.astype(F32)
    bb_re = q_re[..., None] * Br - q_im[..., None] * Bi
    bb_im = q_re[..., None] * Bi + q_im[..., None] * Br

    gb = 128 // N
    eye_g = jnp.eye(gb, dtype=F32)

    def in_blocks(bb):
        return jnp.einsum('kgpn,gh->kgnhp', bb.reshape(G // gb, gb, P, N),
                          eye_g).reshape(G // gb, gb * N, gb * P).astype(BF16)

    def out_blocks(cc):
        return jnp.einsum('kgnp,gh->kgphn', cc.reshape(G // gb, gb, N, P),
                          eye_g).reshape(G // gb, gb * P, gb * N).astype(BF16)

    pr, pi = a_re, a_im
    for _ in range(int(math.log2(SUBLEN))):
        pr, pi = pr * pr - pi * pi, 2.0 * pr * pi
    flat = lambda a: a.reshape(1, G * P)
    return (in_blocks(bb_re), in_blocks(bb_im), flat(a_re), flat(a_im), flat(pr), flat(pi),
            out_blocks(C_re.astype(F32)), out_blocks(-C_im.astype(F32)))


def kernel(x, norm_pre_g, w_in, conv_w, conv_b, w_q, w_k, w_v, b_i, b_f, head_g, skip_a,
           w_a_out, lam_re, lam_im, log_dt, B_re, B_im, C_re, C_im, D_skip, w_glu, b_glu,
           w_b_out, w_o, norm_post_g):
    batch, seq, _ = x.shape
    depth = w_in.shape[0]
    T = batch * seq
    h = x.reshape(T, D_MODEL)

    p = np.arange(TILE)
    src = (p % SUBSEQ) * SUBLEN + p // SUBSEQ
    perm_np = np.zeros((TILE, TILE), np.float32)
    perm_np[p, src] = 1.0
    perm = jnp.asarray(perm_np, BF16)
    perm_t = jnp.asarray(perm_np.T, BF16)

    for l in range(depth):
        w = w_in[l]
        w_a = w[:, 0:HALF_COLS].astype(BF16)
        w_b = w[:, HALF_COLS + 8:2 * HALF_COLS + 8].astype(BF16)
        w_gate_t = jnp.pad(w[:, HALF_COLS:HALF_COLS + 128][:, 0:8].T.astype(BF16),
                           ((0, GATE_ROWS - 8), (0, 0)))
        gbias = jnp.zeros((GATE_ROWS, 1), F32).at[0:4, 0].set(b_i[l]).at[4:8, 0].set(b_f[l])
        bin_re, bin_im, a_re, a_im, ap_re, ap_im, cout_re, cout_im = _s5_params(
            lam_re[l], lam_im[l], log_dt[l], B_re[l], B_im[l], C_re[l], C_im[l])
        weights = (
            norm_pre_g[l][None, :], w_a, w_b, w_gate_t,
            conv_w[l], conv_b[l][None, :], _to_bf16(w_q[l]),
            _to_bf16(w_k[l] * (HEAD_DIM ** -0.5)), _to_bf16(w_v[l]), gbias,
            head_g[l][None, :], skip_a[l][None, :], _to_bf16(w_a_out[l]),
            _to_bf16(perm), _to_bf16(perm_t), _to_bf16(bin_re), _to_bf16(bin_im), a_re, a_im, ap_re, ap_im,
            _to_bf16(cout_re), _to_bf16(cout_im),
            D_skip[l].reshape(1, S5_WIDTH), _to_bf16(w_glu[l]), b_glu[l][None, :],
            _to_bf16(w_b_out[l]),
            _to_bf16(w_o[l]), norm_post_g[l][None, :],
        )
        h = _block(h, seq // TILE, weights)
    return h.reshape(batch, seq, D_MODEL)
```

```python
import functools
import math

import jax
import jax.numpy as jnp
import numpy as np
from jax import lax
from jax.experimental import pallas as pl
from jax.experimental.pallas import tpu as pltpu

F32 = jnp.float32
BF16 = jnp.bfloat16

D_MODEL = 1024
HEADS = 4
HEAD_DIM = 256
CHUNK = 256
CONV_WIDTH = 4
S5_WIDTH = 512
S5_GROUP = 16
S5_GROUPS = 32
S5_STATE = 64
S5_LANES = S5_GROUPS * S5_STATE
NORM_EPS = 1e-6
GATE_ROWS = 16

COL_UA, COL_ZA, COL_OA, COL_UB, COL_ZB, COL_G = 0, 1024, 2048, 3072, 3584, 4096
MAIN_COLS = 6144
HALF_COLS = 3072
PROJ_BLOCK = 1024

TILE = 256
SUBSEQ = 8
SUBLEN = TILE // SUBSEQ
S5_LANE_BLOCK = 1024
VMEM_LIMIT = 60 * 1024 * 1024


def _dot(a, b):
    return jnp.dot(a, b, preferred_element_type=F32)


def _dot_nt(a, b):
    return lax.dot_general(a, b, (((1,), (1,)), ((), ())), preferred_element_type=F32)


def _dot_tn(a, b):
    return lax.dot_general(a, b, (((0,), (0,)), ((), ())), preferred_element_type=F32)


def _to_bf16(w):
    return w.astype(BF16)


def _silu(x):
    return x * jax.nn.sigmoid(x)


def _rms(x):
    return x * lax.rsqrt(jnp.mean(x * x, axis=-1, keepdims=True) + NORM_EPS)


def _lane_cumsum(x):
    lane = lax.broadcasted_iota(jnp.int32, x.shape, 1)
    d = 1
    while d < x.shape[1]:
        x = x + jnp.where(lane >= d, pltpu.roll(x, d, axis=1), 0.0)
        d *= 2
    return x


class _Proj:
    def __init__(self, x_ref, g_ref, w_refs, wgt_ref, xn_ref, proj_ref, gates_ref, slot):
        xn_ref[...] = (_rms(x_ref[...]) * g_ref[...]).astype(BF16)
        self.w_refs, self.wgt_ref, self.xn_ref = w_refs, wgt_ref, xn_ref
        self.proj_ref, self.gates_ref, self.slot = proj_ref, gates_ref, slot

    def block(self, j):
        w_ref = self.w_refs[j * PROJ_BLOCK // HALF_COLS]
        wcols = slice(j * PROJ_BLOCK % HALF_COLS, j * PROJ_BLOCK % HALF_COLS + PROJ_BLOCK)
        cols = slice(j * PROJ_BLOCK, (j + 1) * PROJ_BLOCK)
        self.proj_ref[self.slot, :, cols] = _dot(self.xn_ref[...], w_ref[:, wcols]).astype(BF16)

    def gates(self):
        self.gates_ref[self.slot] = _dot_nt(self.wgt_ref[...], self.xn_ref[...])


def _mlstm_conv(proj_ref, slot, seq_start, convw_ref, convb_ref, ext_ref, c_ref, cb_ref):
    ext_ref[0:8, :] = jnp.where(seq_start, 0.0, ext_ref[TILE:TILE + 8, :])
    ext_ref[8:8 + TILE, :] = proj_ref[slot, :, COL_UA:COL_UA + D_MODEL].astype(F32)
    for h in range(HEADS):
        cols = slice(h * HEAD_DIM, (h + 1) * HEAD_DIM)
        acc = convb_ref[:, cols]
        for j in range(CONV_WIDTH):
            shift = CONV_WIDTH - 1 - j
            acc = acc + convw_ref[j:j + 1, cols] * ext_ref[8 - shift:8 - shift + TILE, cols]
        c = _silu(acc)
        c_ref[:, cols] = c
        cb_ref[:, cols] = c.astype(BF16)


def _mlstm_qkv(proj_ref, slot, wq_ref, wk_ref, wv_ref, cb_ref, q_ref, k_ref, v_ref):
    for h in range(HEADS):
        cols = slice(h * HEAD_DIM, (h + 1) * HEAD_DIM)
        cb = cb_ref[:, cols]
        q_ref[:, cols] = _dot(cb, wq_ref[h]).astype(BF16)
        k_ref[:, cols] = _dot(cb, wk_ref[h]).astype(BF16)
        ua = proj_ref[slot, :, COL_UA + h * HEAD_DIM:COL_UA + (h + 1) * HEAD_DIM]
        v_ref[:, cols] = _dot(ua, wv_ref[h]).astype(BF16)


def _mlstm_chunk(c_idx, proj_ref, gates_ref, slot, gbias_ref, headg_ref, skip_ref,
                 c_ref, q_ref, k_ref, v_ref, hfin_ref, ct_ref, n_ref, m_ref):
    row = lax.broadcasted_iota(jnp.int32, (CHUNK, CHUNK), 0)
    col = lax.broadcasted_iota(jnp.int32, (CHUNK, CHUNK), 1)
    eye = row == col
    causal = row >= col
    grow = lax.broadcasted_iota(jnp.int32, (GATE_ROWS, CHUNK), 0)
    glane = lax.broadcasted_iota(jnp.int32, (1, CHUNK), 1)

    rows = slice(c_idx * CHUNK, (c_idx + 1) * CHUNK)
    pre = gates_ref[slot, :, rows] + gbias_ref[...]
    logsig = jnp.minimum(pre, 0.0) - jnp.log1p(jnp.exp(-jnp.abs(pre)))
    gate = jnp.where(grow >= HEADS, logsig, pre)
    bsum = _lane_cumsum(gate)
    for h in range(HEADS):
        cols = slice(h * HEAD_DIM, (h + 1) * HEAD_DIM)
        ig_r = gate[h:h + 1, :]
        b_r = bsum[HEADS + h:HEADS + h + 1, :]
        b_c = jnp.sum(jnp.where(eye, b_r, 0.0), axis=-1, keepdims=True)
        ig_c = jnp.sum(jnp.where(eye, ig_r, 0.0), axis=-1, keepdims=True)
        b_last = jnp.sum(jnp.where(glane == CHUNK - 1, b_r, 0.0), axis=-1, keepdims=True)
        m_prev = m_ref[h]
        q = q_ref[rows, cols]
        k = k_ref[rows, cols]
        v = v_ref[rows, cols]

        log_d = jnp.where(causal, b_c - b_r + ig_r, -jnp.inf)
        log_inter = b_c + m_prev
        m_t = jnp.maximum(log_inter, jnp.max(log_d, axis=-1, keepdims=True))
        dmat = jnp.exp(log_d - m_t)
        inter = jnp.exp(log_inter - m_t)
        s = _dot_nt(q, k) * dmat
        ct_prev = ct_ref[h]
        num = _dot(s.astype(BF16), v) + inter * _dot(q, ct_prev.astype(BF16))
        qn = jnp.sum(q.astype(F32) * n_ref[h], axis=-1, keepdims=True)
        den = jnp.sum(s, axis=-1, keepdims=True) + inter * qn
        hh = num * (1.0 / jnp.maximum(jnp.abs(den), jnp.exp(-m_t)))

        w_end_r = b_last - b_r + ig_r
        m_loc = jnp.max(w_end_r, axis=-1, keepdims=True)
        m_new = jnp.maximum(b_last + m_prev, m_loc)
        decay = jnp.exp(b_last + m_prev - m_new)
        e_c = jnp.exp(b_last - b_c + ig_c - m_new)
        kw = k.astype(F32) * e_c
        ct_ref[h] = decay * ct_prev + _dot_tn(kw.astype(BF16), v)
        n_ref[h] = decay * n_ref[h] + jnp.sum(kw, axis=0, keepdims=True)
        m_ref[h] = m_new

        o_pre = proj_ref[slot, rows, COL_OA + h * HEAD_DIM:COL_OA + (h + 1) * HEAD_DIM]
        z_pre = proj_ref[slot, rows, COL_ZA + h * HEAD_DIM:COL_ZA + (h + 1) * HEAD_DIM]
        hh = _rms(jax.nn.sigmoid(o_pre.astype(F32)) * hh)
        hh = hh * headg_ref[:, cols] + skip_ref[:, cols] * c_ref[rows, cols]
        hh = hh * _silu(z_pre.astype(F32))
        hfin_ref[rows, cols] = hh.astype(BF16)


def _s5_inputs(proj_ref, slot, perm_ref, bre_ref, bim_ref, sre_ref, sim_ref):
    u_perm = _dot(perm_ref[...], proj_ref[slot, :, COL_UB:COL_UB + S5_WIDTH]).astype(BF16)
    for kb in range(S5_WIDTH // 128):
        ukb = u_perm[:, kb * 128:(kb + 1) * 128]
        sre_ref[:, kb * 512:(kb + 1) * 512] = _dot(ukb, bre_ref[kb])
        sim_ref[:, kb * 512:(kb + 1) * 512] = _dot(ukb, bim_ref[kb])


def _s5_scan(lb, are_ref, aim_ref, apre_ref, apim_ref, sre_ref, sim_ref, xre_ref, xim_ref):
    lanes = slice(lb * S5_LANE_BLOCK, (lb + 1) * S5_LANE_BLOCK)
    shape = (SUBSEQ, S5_LANE_BLOCK)
    ar = jnp.broadcast_to(are_ref[:, lanes], shape)
    ai = jnp.broadcast_to(aim_ref[:, lanes], shape)

    def step(j, sr, si):
        r8 = slice(j * SUBSEQ, (j + 1) * SUBSEQ)
        br = sre_ref[r8, lanes]
        bi = sim_ref[r8, lanes]
        return ar * sr - ai * si + br, ar * si + ai * sr + bi

    er = jnp.zeros(shape, F32)
    ei = jnp.zeros(shape, F32)
    for j in range(SUBLEN):
        er, ei = step(j, er, ei)

    sub = lax.broadcasted_iota(jnp.int32, shape, 0)
    pr = apre_ref[:, lanes]
    pi = apim_ref[:, lanes]
    xr = xre_ref[:, lanes]
    xi = xim_ref[:, lanes]
    sr = jnp.zeros(shape, F32)
    si = jnp.zeros(shape, F32)
    for r in range(SUBSEQ):
        sr = jnp.where(sub == r, xr, sr)
        si = jnp.where(sub == r, xi, si)
        xr, xi = (pr * xr - pi * xi + er[r:r + 1, :], pr * xi + pi * xr + ei[r:r + 1, :])
    xre_ref[:, lanes] = xr
    xim_ref[:, lanes] = xi

    for j in range(SUBLEN):
        sr, si = step(j, sr, si)
        r8 = slice(j * SUBSEQ, (j + 1) * SUBSEQ)
        sre_ref[r8, lanes] = sr
        sim_ref[r8, lanes] = si


def _s5_outputs(proj_ref, slot, permt_ref, cre_ref, cim_ref, dskip_ref, wglu_ref, bglu_ref,
                wbout_ref, sre_ref, sim_ref, yperm_ref):
    for ob in range(S5_WIDTH // 128):
        lanes = slice(ob * 512, (ob + 1) * 512)
        yperm_ref[:, ob * 128:(ob + 1) * 128] = (
            _dot(sre_ref[:, lanes].astype(BF16), cre_ref[ob])
            + _dot(sim_ref[:, lanes].astype(BF16), cim_ref[ob])).astype(BF16)
    u = proj_ref[slot, :, COL_UB:COL_UB + S5_WIDTH].astype(F32)
    y = _dot(permt_ref[...], yperm_ref[...]) + dskip_ref[...] * u
    y = 0.5 * y * (1.0 + jnp.tanh(math.sqrt(2.0 / math.pi) * (y + 0.044715 * (y * y * y))))
    y = y * jax.nn.sigmoid(_dot(y.astype(BF16), wglu_ref[...]) + bglu_ref[...])
    y = y * _silu(proj_ref[slot, :, COL_ZB:COL_ZB + S5_WIDTH].astype(F32))
    return _dot(y.astype(BF16), wbout_ref[...])


def _block_kernel(nt,
                  xcur_ref, xnxt_ref, gpre_ref, wa_ref, wb_ref, wgt_ref,
                  convw_ref, convb_ref, wq_ref, wk_ref, wv_ref, gbias_ref, headg_ref, skip_ref,
                  waout_ref,
                  perm_ref, permt_ref, bre_ref, bim_ref, are_ref, aim_ref, apre_ref, apim_ref,
                  cre_ref, cim_ref, dskip_ref, wglu_ref, bglu_ref, wbout_ref,
                  wo_ref, gpost_ref,
                  out_ref,
                  xn_ref, proj_ref, gates_ref,
                  ext_ref, c_ref, cb_ref, q_ref, k_ref, v_ref, hfin_ref, ct_ref, n_ref, m_ref,
                  sre_ref, sim_ref, xre_ref, xim_ref, yperm_ref):
    step = pl.program_id(0)
    slot = step % 2
    nslot = 1 - slot

    @pl.when(step == 0)
    def _():
        first = _Proj(xcur_ref, gpre_ref, (wa_ref, wb_ref), wgt_ref, xn_ref, proj_ref, gates_ref, 0)
        for j in range(MAIN_COLS // PROJ_BLOCK):
            first.block(j)
        first.gates()
        ext_ref[TILE:TILE + 8, :] = jnp.zeros((8, D_MODEL), F32)
        _mlstm_conv(proj_ref, 0, True, convw_ref, convb_ref, ext_ref, c_ref, cb_ref)

    @pl.when(step % nt == 0)
    def _():
        ct_ref[...] = jnp.zeros_like(ct_ref)
        n_ref[...] = jnp.zeros_like(n_ref)
        m_ref[...] = jnp.full(m_ref.shape, -jnp.inf, F32)
        xre_ref[...] = jnp.zeros_like(xre_ref)
        xim_ref[...] = jnp.zeros_like(xim_ref)

    nxt = _Proj(xnxt_ref, gpre_ref, (wa_ref, wb_ref), wgt_ref, xn_ref, proj_ref, gates_ref, nslot)

    scan_args = (are_ref, aim_ref, apre_ref, apim_ref, sre_ref, sim_ref, xre_ref, xim_ref)
    chunk_args = (proj_ref, gates_ref, slot, gbias_ref, headg_ref, skip_ref,
                  c_ref, q_ref, k_ref, v_ref, hfin_ref, ct_ref, n_ref, m_ref)
    _s5_inputs(proj_ref, slot, perm_ref, bre_ref, bim_ref, sre_ref, sim_ref)
    _mlstm_qkv(proj_ref, slot, wq_ref, wk_ref, wv_ref, cb_ref, q_ref, k_ref, v_ref)
    nxt.block(0)
    nxt.block(1)
    nxt.block(2)
    nxt.block(3)
    _s5_scan(0, *scan_args)
    _s5_scan(1, *scan_args)
    _mlstm_chunk(0, *chunk_args)
    _mlstm_conv(proj_ref, nslot, (step + 1) % nt == 0, convw_ref, convb_ref, ext_ref, c_ref, cb_ref)
    nxt.block(4)
    nxt.block(5)
    nxt.gates()
    y_a = _dot(hfin_ref[...], waout_ref[...])
    y_b = _s5_outputs(proj_ref, slot, permt_ref, cre_ref, cim_ref, dskip_ref, wglu_ref, bglu_ref,
                      wbout_ref, sre_ref, sim_ref, yperm_ref)

    ga = jax.nn.sigmoid(proj_ref[slot, :, COL_G:COL_G + D_MODEL].astype(F32))
    gb = jax.nn.sigmoid(proj_ref[slot, :, COL_G + D_MODEL:COL_G + 2 * D_MODEL].astype(F32))
    merged = (ga * y_a + gb * y_b).astype(BF16)
    out_ref[...] = xcur_ref[...] + _rms(_dot(merged, wo_ref[...])) * gpost_ref[...]


def _block(x2d, nt, weights):
    T = x2d.shape[0]
    nsteps = T // TILE

    def resident(a):
        nd = a.ndim
        return pl.BlockSpec(a.shape, lambda s: (0,) * nd, pipeline_mode=pl.Buffered(1))

    return pl.pallas_call(
        functools.partial(_block_kernel, nt),
        grid=(nsteps,),
        in_specs=[
            pl.BlockSpec((TILE, D_MODEL), lambda s: (s, 0)),
            pl.BlockSpec((TILE, D_MODEL), lambda s: (jnp.minimum(s + 1, nsteps - 1), 0)),
        ] + [resident(w) for w in weights],
        out_specs=pl.BlockSpec((TILE, D_MODEL), lambda s: (s, 0)),
        out_shape=jax.ShapeDtypeStruct((T, D_MODEL), F32),
        scratch_shapes=[
            pltpu.VMEM((TILE, D_MODEL), BF16),
            pltpu.VMEM((2, TILE, MAIN_COLS), BF16),
            pltpu.VMEM((2, GATE_ROWS, TILE), F32),
            pltpu.VMEM((TILE + 8, D_MODEL), F32),
            pltpu.VMEM((TILE, D_MODEL), F32),
            pltpu.VMEM((TILE, D_MODEL), BF16),
            pltpu.VMEM((TILE, D_MODEL), BF16),
            pltpu.VMEM((TILE, D_MODEL), BF16),
            pltpu.VMEM((TILE, D_MODEL), BF16),
            pltpu.VMEM((TILE, D_MODEL), BF16),
            pltpu.VMEM((HEADS, HEAD_DIM, HEAD_DIM), F32),
            pltpu.VMEM((HEADS, 1, HEAD_DIM), F32),
            pltpu.VMEM((HEADS, 1, 1), F32),
            pltpu.VMEM((TILE, S5_LANES), F32),
            pltpu.VMEM((TILE, S5_LANES), F32),
            pltpu.VMEM((1, S5_LANES), F32),
            pltpu.VMEM((1, S5_LANES), F32),
            pltpu.VMEM((TILE, S5_WIDTH), BF16),
        ],
        compiler_params=pltpu.CompilerParams(
            dimension_semantics=("arbitrary",), vmem_limit_bytes=VMEM_LIMIT),
        name="hybrid_block",
    )(x2d, x2d, *weights)


def _s5_params(lam_re, lam_im, log_dt, B_re, B_im, C_re, C_im):
    G, P, N = S5_GROUPS, S5_STATE, S5_GROUP
    dt = jnp.exp(log_dt.astype(F32))[:, None]
    lr = lam_re.astype(F32)
    li = lam_im.astype(F32)
    mag = jnp.exp(lr * dt)
    a_re = mag * jnp.cos(li * dt)
    a_im = mag * jnp.sin(li * dt)
    den = lr * lr + li * li
    nr = a_re - 1.0
    q_re = (nr * lr + a_im * li) / den
    q_im = (a_im * lr - nr * li) / den
    Br = B_re.astype(F32)
    Bi = B_im.astype(F32)
    bb_re = q_re[..., None] * Br - q_im[..., None] * Bi
    bb_im = q_re[..., None] * Bi + q_im[..., None] * Br

    gb = 128 // N
    eye_g = jnp.eye(gb, dtype=F32)

    def in_blocks(bb):
        return jnp.einsum('kgpn,gh->kgnhp', bb.reshape(G // gb, gb, P, N),
                          eye_g).reshape(G // gb, gb * N, gb * P).astype(BF16)

    def out_blocks(cc):
        return jnp.einsum('kgnp,gh->kgphn', cc.reshape(G // gb, gb, N, P),
                          eye_g).reshape(G // gb, gb * P, gb * N).astype(BF16)

    pr, pi = a_re, a_im
    for _ in range(int(math.log2(SUBLEN))):
        pr, pi = pr * pr - pi * pi, 2.0 * pr * pi
    flat = lambda a: a.reshape(1, G * P)
    return (in_blocks(bb_re), in_blocks(bb_im), flat(a_re), flat(a_im), flat(pr), flat(pi),
            out_blocks(C_re.astype(F32)), out_blocks(-C_im.astype(F32)))


def kernel(x, norm_pre_g, w_in, conv_w, conv_b, w_q, w_k, w_v, b_i, b_f, head_g, skip_a,
           w_a_out, lam_re, lam_im, log_dt, B_re, B_im, C_re, C_im, D_skip, w_glu, b_glu,
           w_b_out, w_o, norm_post_g):
    batch, seq, _ = x.shape
    depth = w_in.shape[0]
    T = batch * seq
    h = x.reshape(T, D_MODEL)

    p = np.arange(TILE)
    src = (p % SUBSEQ) * SUBLEN + p // SUBSEQ
    perm_np = np.zeros((TILE, TILE), np.float32)
    perm_np[p, src] = 1.0
    perm = jnp.asarray(perm_np, BF16)
    perm_t = jnp.asarray(perm_np.T, BF16)

    for l in range(depth):
        w = w_in[l]
        w_a = w[:, 0:HALF_COLS].astype(BF16)
        w_b = w[:, HALF_COLS + 8:2 * HALF_COLS + 8].astype(BF16)
        w_gate_t = jnp.pad(w[:, HALF_COLS:HALF_COLS + 128][:, 0:8].T.astype(BF16),
                           ((0, GATE_ROWS - 8), (0, 0)))
        gbias = jnp.zeros((GATE_ROWS, 1), F32).at[0:4, 0].set(b_i[l]).at[4:8, 0].set(b_f[l])
        bin_re, bin_im, a_re, a_im, ap_re, ap_im, cout_re, cout_im = _s5_params(
            lam_re[l], lam_im[l], log_dt[l], B_re[l], B_im[l], C_re[l], C_im[l])
        weights = (
            norm_pre_g[l][None, :], w_a, w_b, w_gate_t,
            conv_w[l], conv_b[l][None, :], _to_bf16(w_q[l]),
            _to_bf16(w_k[l] * (HEAD_DIM ** -0.5)), _to_bf16(w_v[l]), gbias,
            head_g[l][None, :], skip_a[l][None, :], _to_bf16(w_a_out[l]),
            _to_bf16(perm), _to_bf16(perm_t), _to_bf16(bin_re), _to_bf16(bin_im), a_re, a_im, ap_re, ap_im,
            _to_bf16(cout_re), _to_bf16(cout_im),
            D_skip[l].reshape(1, S5_WIDTH), _to_bf16(w_glu[l]), b_glu[l][None, :],
            _to_bf16(w_b_out[l]),
            _to_bf16(w_o[l]), norm_post_g[l][None, :],
        )
        h = _block(h, seq // TILE, weights)
    return h.reshape(batch, seq, D_MODEL)
```
